```python
import math
import jax, jax.numpy as jnp
from jax import lax
import numpy as np

D_MODEL = 2048
BATCH = 4
SEQ = 2048
DEPTH = 4

N_MIXERS = 2
ATTN_HEADS = 8
ATTN_HEAD_DIM = 128
ATTN_V_DIM = 2 * ATTN_HEAD_DIM
ATTN_QK_WIDTH = ATTN_HEADS * 2 * ATTN_HEAD_DIM
ATTN_WIDTH = ATTN_HEADS * ATTN_V_DIM
CONV_WIDTH = D_MODEL
CONV_KERNEL = 31
REL_BUCKETS = 32
REL_MAX_DIST = 128
Q_BLOCK = 128
EPS = 1e-6
LN_EPS = 1e-5
N_ATTN_LAYERS = (DEPTH + 1) // 2
N_CONV_LAYERS = DEPTH // 2

kernel_name = "hybrid_diffattn_conformer_encoder"


def _rmsnorm(x, gain):
    x32 = x.astype(jnp.float32)
    y = x32 * lax.rsqrt(jnp.mean(x32 * x32, axis=-1, keepdims=True) + EPS)
    return (y * gain.astype(jnp.float32)).astype(x.dtype)


def _rel_bucket(rel):
    nb = REL_BUCKETS // 2
    max_exact = nb // 2
    ret = jnp.where(rel > 0, nb, 0)
    n = jnp.abs(rel)
    nf = jnp.maximum(n, 1).astype(jnp.float32)
    large = max_exact + (jnp.log(nf / max_exact) / math.log(REL_MAX_DIST / max_exact)
                         * (nb - max_exact)).astype(jnp.int32)
    large = jnp.minimum(large, nb - 1)
    return ret + jnp.where(n < max_exact, n, large)


def _lambda_init(layer_idx):
    return 0.8 - 0.6 * math.exp(-0.3 * layer_idx)


def _diff_attention(xn, w_in, w_out, lq1, lk1, lq2, lk2, head_gain, rel_bias, lambda_init):
    B, S, _ = xn.shape
    h = xn @ w_in
    q, k, v, gate = jnp.split(h, [ATTN_QK_WIDTH, 2 * ATTN_QK_WIDTH,
                                  2 * ATTN_QK_WIDTH + ATTN_WIDTH], axis=-1)
    q = q.reshape(B, S, ATTN_HEADS, 2, ATTN_HEAD_DIM) * (ATTN_HEAD_DIM ** -0.5)
    k = k.reshape(B, S, ATTN_HEADS, 2, ATTN_HEAD_DIM)
    v = v.reshape(B, S, ATTN_HEADS, ATTN_V_DIM)
    f32 = jnp.float32
    lam = (jnp.exp(jnp.sum(lq1.astype(f32) * lk1.astype(f32)))
           - jnp.exp(jnp.sum(lq2.astype(f32) * lk2.astype(f32))) + lambda_init)
    k_pos = jnp.arange(S, dtype=jnp.int32)

    def block(n):
        start = n * Q_BLOCK
        qb = lax.dynamic_slice_in_dim(q, start, Q_BLOCK, axis=1)
        q_pos = start + jnp.arange(Q_BLOCK, dtype=jnp.int32)
        bias = rel_bias[_rel_bucket(k_pos[None, :] - q_pos[:, None])]
        bias = jnp.transpose(bias, (2, 0, 1)).astype(f32)
        s = jnp.einsum("bqhcd,bkhcd->bhcqk", qb, k).astype(f32) + bias[None, :, None]
        p = jax.nn.softmax(s, axis=-1)
        a = p[:, :, 0] - lam * p[:, :, 1]
        return jnp.einsum("bhqk,bkhe->bqhe", a.astype(v.dtype), v)

    o = lax.map(block, jnp.arange(S // Q_BLOCK))
    o = jnp.moveaxis(o, 0, 1).reshape(B, S, ATTN_HEADS, ATTN_V_DIM)
    o32 = o.astype(f32)
    o32 = (o32 * lax.rsqrt(jnp.mean(o32 * o32, axis=-1, keepdims=True) + EPS)
           * head_gain.astype(f32) * (1.0 - lambda_init))
    o = o32.astype(xn.dtype).reshape(B, S, ATTN_WIDTH)
    return (o * jax.nn.silu(gate)) @ w_out


def _conformer_conv(xn, w_in, b_in, dw, dw_b, ln_g, ln_b, w_out, b_out):
    h = xn @ w_in + b_in
    a, g, z = jnp.split(h, [CONV_WIDTH, 2 * CONV_WIDTH], axis=-1)
    u = a * jax.nn.sigmoid(g)
    u = lax.conv_general_dilated(
        u, dw.reshape(CONV_KERNEL, 1, CONV_WIDTH).astype(u.dtype),
        window_strides=(1,), padding=((CONV_KERNEL // 2, CONV_KERNEL // 2),),
        dimension_numbers=("NWC", "WIO", "NWC"),
        feature_group_count=CONV_WIDTH) + dw_b
    u32 = u.astype(jnp.float32)
    mu = jnp.mean(u32, axis=-1, keepdims=True)
    var = jnp.mean(jnp.square(u32 - mu), axis=-1, keepdims=True)
    u32 = (u32 - mu) * lax.rsqrt(var + LN_EPS) * ln_g.astype(jnp.float32) + ln_b.astype(jnp.float32)
    u = jax.nn.silu(u32).astype(xn.dtype)
    return (u * jax.nn.silu(z)) @ w_out + b_out


def setup_inputs(seed: int = 0) -> dict:
    key = jax.random.key(seed)
    ks = jax.random.split(key, 20)
    nrm = jax.random.normal
    D, NA, NC = D_MODEL, N_ATTN_LAYERS, N_CONV_LAYERS
    attn_in_w = 2 * ATTN_QK_WIDTH + 2 * ATTN_WIDTH
    return {
        "x": nrm(ks[0], (BATCH, SEQ, D), jnp.float32),
        "norm_gain": 1.0 + 0.05 * nrm(ks[1], (DEPTH, D), jnp.float32),
        "final_gain": 1.0 + 0.05 * nrm(ks[2], (D,), jnp.float32),
        "rel_bias": 0.5 * nrm(ks[3], (REL_BUCKETS, ATTN_HEADS), jnp.float32),
        "attn_w_in": nrm(ks[4], (NA, D, attn_in_w), jnp.float32) * D ** -0.5,
        "attn_w_out": nrm(ks[5], (NA, ATTN_WIDTH, D), jnp.float32) * ATTN_WIDTH ** -0.5,
        "attn_lq1": 0.1 * nrm(ks[6], (NA, ATTN_HEAD_DIM), jnp.float32),
        "attn_lk1": 0.1 * nrm(ks[7], (NA, ATTN_HEAD_DIM), jnp.float32),
        "attn_lq2": 0.1 * nrm(ks[8], (NA, ATTN_HEAD_DIM), jnp.float32),
        "attn_lk2": 0.1 * nrm(ks[9], (NA, ATTN_HEAD_DIM), jnp.float32),
        "attn_head_gain": 1.0 + 0.05 * nrm(ks[10], (NA, ATTN_HEADS, ATTN_V_DIM), jnp.float32),
        "conv_w_in": nrm(ks[11], (NC, D, 3 * CONV_WIDTH), jnp.float32) * D ** -0.5,
        "conv_b_in": 0.02 * nrm(ks[12], (NC, 3 * CONV_WIDTH), jnp.float32),
        "conv_dw": nrm(ks[13], (NC, CONV_KERNEL, CONV_WIDTH), jnp.float32) * CONV_KERNEL ** -0.5,
        "conv_dw_b": 0.02 * nrm(ks[14], (NC, CONV_WIDTH), jnp.float32),
        "conv_ln_g": 1.0 + 0.05 * nrm(ks[15], (NC, CONV_WIDTH), jnp.float32),
        "conv_ln_b": 0.02 * nrm(ks[16], (NC, CONV_WIDTH), jnp.float32),
        "conv_w_out": nrm(ks[17], (NC, CONV_WIDTH, D), jnp.float32) * CONV_WIDTH ** -0.5,
        "conv_b_out": 0.02 * nrm(ks[18], (NC, D), jnp.float32),
    }


def reference(x, norm_gain, final_gain, rel_bias, attn_w_in, attn_w_out, attn_lq1, attn_lk1,
              attn_lq2, attn_lk2, attn_head_gain, conv_w_in, conv_b_in, conv_dw, conv_dw_b,
              conv_ln_g, conv_ln_b, conv_w_out, conv_b_out):
    for i in range(DEPTH):
        xn = _rmsnorm(x, norm_gain[i])
        j = i // N_MIXERS
        if i % N_MIXERS == 0:
            y = _diff_attention(xn, attn_w_in[j], attn_w_out[j], attn_lq1[j], attn_lk1[j],
                                attn_lq2[j], attn_lk2[j], attn_head_gain[j], rel_bias,
                                _lambda_init(i))
        else:
            y = _conformer_conv(xn, conv_w_in[j], conv_b_in[j], conv_dw[j], conv_dw_b[j],
                                conv_ln_g[j], conv_ln_b[j], conv_w_out[j], conv_b_out[j])
        x = x + y
    return _rmsnorm(x, final_gain)
```

```python
import functools
import math

import jax
import jax.numpy as jnp
from jax import lax
from jax.experimental import pallas as pl
from jax.experimental.pallas import tpu as pltpu

F32 = jnp.float32
BF16 = jnp.bfloat16

D_MODEL = 2048
DEPTH = 4
HEADS = 8
HEAD_DIM = 128
V_DIM = 2 * HEAD_DIM
QK_WIDTH = HEADS * 2 * HEAD_DIM
ATTN_WIDTH = HEADS * V_DIM
CONV_WIDTH = D_MODEL
CONV_KERNEL = 31
CONV_HALF = CONV_KERNEL // 2
REL_BUCKETS = 32
EPS = 1e-6
LN_EPS = 1e-5

V7X_LANES = 128
V7X_SUBLANES = 8
V7X_VMEM_BYTES = 64 * 1024 * 1024
VMEM_CAP_BYTES = 56 * 1024 * 1024

TQ = 256
BAND = 2 * TQ
BAND_TILE = BAND + TQ
HALO = 2 * V7X_SUBLANES
BIAS_SATURATION = 91
assert TQ // 2 >= BIAS_SATURATION and HALO >= CONV_HALF


def _lambda_init(layer_idx):
    return 0.8 - 0.6 * math.exp(-0.3 * layer_idx)


def _vmem_limit(nbytes):
    return int(min(VMEM_CAP_BYTES, nbytes))


def _params(vmem_bytes, ndim):
    return pltpu.CompilerParams(
        dimension_semantics=("arbitrary",) * ndim,
        vmem_limit_bytes=_vmem_limit(vmem_bytes),
    )


def _rmsnorm_rows(x, gain):
    ms = jnp.mean(x * x, axis=-1, keepdims=True)
    return x * lax.rsqrt(ms + EPS) * gain


def _silu(x):
    return x * jax.nn.sigmoid(x)


def _bucket(rel):
    n = jnp.abs(rel)
    large = jnp.full(rel.shape, 8, jnp.int32)
    for thr in (12, 16, 23, 32, 46, 64, BIAS_SATURATION):
        large = large + (n >= thr).astype(jnp.int32)
    return jnp.where(rel > 0, 16, 0) + jnp.where(n < 8, n, large)


def _bias_band_kernel(tab_ref, o_ref):
    h = pl.program_id(0)
    ql = lax.broadcasted_iota(jnp.int32, (TQ, BAND_TILE), 0)
    c = lax.broadcasted_iota(jnp.int32, (TQ, BAND_TILE), 1)
    idx = _bucket(c - TQ - ql)
    acc = jnp.full((TQ, BAND_TILE), tab_ref[h, 0], F32)
    for b in range(1, REL_BUCKETS):
        acc = jnp.where(idx == b, tab_ref[h, b], acc)
    o_ref[...] = acc


def _bias_band(tab):
    return pl.pallas_call(
        _bias_band_kernel,
        grid=(HEADS,),
        in_specs=[pl.BlockSpec(memory_space=pltpu.SMEM)],
        out_specs=pl.BlockSpec((None, TQ, BAND_TILE), lambda h: (h, 0, 0)),
        out_shape=jax.ShapeDtypeStruct((HEADS, TQ, BAND_TILE), F32),
        compiler_params=_params(8 * TQ * BAND_TILE * 4, 1),
        name="bias_band",
    )(tab)


def _rms_proj_kernel(x_ref, g_ref, w_ref, o_ref, xn_ref, *, n_scaled_tiles, col_scale):
    j = pl.program_id(1)

    @pl.when(j == 0)
    def _():
        xn_ref[...] = _rmsnorm_rows(x_ref[...], g_ref[...]).astype(BF16)

    acc = jnp.dot(xn_ref[...], w_ref[...], preferred_element_type=F32)
    scale = jnp.where(j < n_scaled_tiles, col_scale, 1.0).astype(F32)
    o_ref[...] = (acc * scale).astype(o_ref.dtype)


def _rms_proj(x, gain, w, *, n_scaled, col_scale, tm, tn):
    m, d = x.shape
    n = w.shape[1]
    vmem = 2 * tm * d * 4 + tm * d * 2 + 2 * d * tn * 2 + 2 * tm * tn * 2 + 2 * tm * tn * 4 + (2 << 20)
    return pl.pallas_call(
        functools.partial(_rms_proj_kernel, n_scaled_tiles=n_scaled // tn, col_scale=col_scale),
        grid=(m // tm, n // tn),
        in_specs=[
            pl.BlockSpec((tm, d), lambda i, j: (i, 0)),
            pl.BlockSpec((1, d), lambda i, j: (0, 0)),
            pl.BlockSpec((d, tn), lambda i, j: (0, j)),
        ],
        out_specs=pl.BlockSpec((tm, tn), lambda i, j: (i, j)),
        out_shape=jax.ShapeDtypeStruct((m, n), BF16),
        scratch_shapes=[pltpu.VMEM((tm, d), BF16)],
        compiler_params=_params(vmem, 2),
        name="rms_proj",
    )(x, gain, w)


def _attn_kernel(tab_ref, q_ref, k_ref, v_ref, gate_ref, nb_ref, hg_ref, lq1_ref, lk1_ref,
                 lq2_ref, lk2_ref, o_ref, s1_ref, s2_ref, *, seq, lambda_init):
    h = pl.program_id(1)
    i = pl.program_id(2)
    dn = (((1,), (1,)), ((), ()))
    q_start = i * TQ
    ws = jnp.clip(q_start - TQ // 2, 0, seq - BAND)
    boff = ws - q_start + TQ
    ws = pl.multiple_of(ws, V7X_LANES)
    boff = pl.multiple_of(boff, V7X_LANES)

    col = lax.broadcasted_iota(jnp.int32, (1, seq), 1)
    far = jnp.where(col < ws, tab_ref[h, REL_BUCKETS // 2 - 1],
                    jnp.where(col >= ws + BAND, tab_ref[h, REL_BUCKETS - 1], 0.0)).astype(F32)
    band = nb_ref[:, pl.ds(boff, BAND)]

    def probs(lo, s_ref):
        s = lax.dot_general(q_ref[:, lo:lo + HEAD_DIM], k_ref[:, lo:lo + HEAD_DIM], dn,
                            preferred_element_type=F32)
        s_ref[...] = s + far
        s_ref[:, pl.ds(ws, BAND)] += band
        s = s_ref[...]
        p = jnp.exp(s - jnp.max(s, axis=-1, keepdims=True))
        return p, jnp.sum(p, axis=-1, keepdims=True)

    p1, l1 = probs(0, s1_ref)
    p2, l2 = probs(HEAD_DIM, s2_ref)

    lam = (jnp.exp(jnp.sum(lq1_ref[...] * lk1_ref[...], axis=-1, keepdims=True))
           - jnp.exp(jnp.sum(lq2_ref[...] * lk2_ref[...], axis=-1, keepdims=True)) + lambda_init)
    a = p1 * (1.0 / l1) - p2 * (lam / l2)
    o = jnp.dot(a.astype(BF16), v_ref[...], preferred_element_type=F32)
    o = o * lax.rsqrt(jnp.mean(o * o, axis=-1, keepdims=True) + EPS) * hg_ref[...] * (1.0 - lambda_init)
    o_ref[...] = (o * _silu(gate_ref[...].astype(F32))).astype(o_ref.dtype)


def _attention(tab, hproj, nb, head_gain, lq1, lk1, lq2, lk2, *, batch, seq, lambda_init):
    h3 = hproj.reshape(batch, seq, hproj.shape[-1])
    kb, vb, gb = QK_WIDTH // V_DIM, 2 * QK_WIDTH // V_DIM, (2 * QK_WIDTH + ATTN_WIDTH) // V_DIM
    vec = pl.BlockSpec((1, HEAD_DIM), lambda b, h, i: (0, 0))
    vmem = (2 * (2 * TQ * V_DIM * 2 + 2 * seq * V_DIM * 2 + TQ * BAND_TILE * 4 + TQ * V_DIM * 2)
            + 2 * TQ * seq * 4 + 6 * TQ * seq * 4 + (2 << 20))
    return pl.pallas_call(
        functools.partial(_attn_kernel, seq=seq, lambda_init=lambda_init),
        grid=(batch, HEADS, seq // TQ),
        in_specs=[
            pl.BlockSpec(memory_space=pltpu.SMEM),
            pl.BlockSpec((None, TQ, V_DIM), lambda b, h, i: (b, i, h)),
            pl.BlockSpec((None, seq, V_DIM), lambda b, h, i: (b, 0, kb + h)),
            pl.BlockSpec((None, seq, V_DIM), lambda b, h, i: (b, 0, vb + h)),
            pl.BlockSpec((None, TQ, V_DIM), lambda b, h, i: (b, i, gb + h)),
            pl.BlockSpec((None, TQ, BAND_TILE), lambda b, h, i: (h, 0, 0)),
            pl.BlockSpec((None, 1, V_DIM), lambda b, h, i: (h, 0, 0)),
            vec, vec, vec, vec,
        ],
        out_specs=pl.BlockSpec((None, TQ, V_DIM), lambda b, h, i: (b, i, h)),
        out_shape=jax.ShapeDtypeStruct((batch, seq, ATTN_WIDTH), BF16),
        scratch_shapes=[pltpu.VMEM((TQ, seq), F32), pltpu.VMEM((TQ, seq), F32)],
        compiler_params=_params(vmem, 3),
        name="diff_attention",
    )(tab, h3, h3, h3, h3, nb, head_gain.reshape(HEADS, 1, V_DIM), lq1, lk1, lq2, lk2)


def _proj_residual_kernel(y_ref, w_ref, x_ref, o_ref):
    o_ref[...] = x_ref[...] + jnp.dot(y_ref[...], w_ref[...], preferred_element_type=F32)


def _proj_residual(y, w, x, *, tm):
    m, k = y.shape
    n = w.shape[1]
    vmem = 2 * (tm * k * 2 + k * n * 2 + 2 * tm * n * 4) + tm * n * 4 + (2 << 20)
    return pl.pallas_call(
        _proj_residual_kernel,
        grid=(m // tm,),
        in_specs=[
            pl.BlockSpec((tm, k), lambda i: (i, 0)),
            pl.BlockSpec((k, n), lambda i: (0, 0)),
            pl.BlockSpec((tm, n), lambda i: (i, 0)),
        ],
        out_specs=pl.BlockSpec((tm, n), lambda i: (i, 0)),
        out_shape=jax.ShapeDtypeStruct((m, n), F32),
        compiler_params=_params(vmem, 1),
        name="proj_residual",
    )(y, w, x)


def _rms_glu_kernel(x_ref, g_ref, wa_ref, wg_ref, wz_ref, ba_ref, bg_ref, bz_ref, u_ref, sz_ref, xn_ref):
    @pl.when(pl.program_id(1) == 0)
    def _():
        xn_ref[...] = _rmsnorm_rows(x_ref[...], g_ref[...]).astype(BF16)

    xn = xn_ref[...]
    a = jnp.dot(xn, wa_ref[...], preferred_element_type=F32) + ba_ref[...]
    g = jnp.dot(xn, wg_ref[...], preferred_element_type=F32) + bg_ref[...]
    u_ref[...] = a * jax.nn.sigmoid(g)
    z = jnp.dot(xn, wz_ref[...], preferred_element_type=F32) + bz_ref[...]
    sz_ref[...] = _silu(z).astype(sz_ref.dtype)


def _rms_glu(x, gain, w, b, *, tm, tn):
    m, d = x.shape
    c = w.shape[1] // 3
    nj = c // tn
    vmem = (2 * tm * d * 4 + tm * d * 2 + 2 * 3 * d * tn * 2 + 2 * tm * tn * (4 + 2)
            + 4 * tm * tn * 4 + (2 << 20))
    wspec = lambda part: pl.BlockSpec((d, tn), lambda i, j: (0, part * nj + j))
    bspec = lambda part: pl.BlockSpec((1, tn), lambda i, j: (0, part * nj + j))
    return pl.pallas_call(
        _rms_glu_kernel,
        grid=(m // tm, nj),
        in_specs=[
            pl.BlockSpec((tm, d), lambda i, j: (i, 0)),
            pl.BlockSpec((1, d), lambda i, j: (0, 0)),
            wspec(0), wspec(1), wspec(2), bspec(0), bspec(1), bspec(2),
        ],
        out_specs=[pl.BlockSpec((tm, tn), lambda i, j: (i, j))] * 2,
        out_shape=[jax.ShapeDtypeStruct((m, c), F32), jax.ShapeDtypeStruct((m, c), BF16)],
        scratch_shapes=[pltpu.VMEM((tm, d), BF16)],
        compiler_params=_params(vmem, 2),
        name="rms_glu",
    )(x, gain, w, w, w, b, b, b)


def _conv_out_kernel(u_ref, up_ref, un_ref, sz_ref, dw_ref, dwb_ref, lng_ref, lnb_ref, w_ref, bo_ref,
                     x_ref, fg_ref, o_ref, ext_ref, y_ref, *, ts, tiles_per_seq, final_norm):
    i = pl.program_id(0)
    pos = i % tiles_per_seq
    c = u_ref.shape[1]

    ext_ref[0:HALO, :] = jnp.where(pos == 0, 0.0, up_ref[...])
    ext_ref[HALO:HALO + ts, :] = u_ref[...]
    ext_ref[HALO + ts:HALO + ts + HALO, :] = jnp.where(pos == tiles_per_seq - 1, 0.0, un_ref[...])

    def conv_chunk(cc, carry):
        lanes = pl.ds(pl.multiple_of(cc * V7X_LANES, V7X_LANES), V7X_LANES)
        acc = jnp.zeros((ts, V7X_LANES), F32) + dwb_ref[:, lanes]
        for t in range(CONV_KERNEL):
            acc = acc + ext_ref[pl.ds(HALO - CONV_HALF + t, ts), lanes] * dw_ref[t:t + 1, lanes]
        y_ref[:, lanes] = acc
        return carry

    lax.fori_loop(0, c // V7X_LANES, conv_chunk, 0)

    y = y_ref[...]
    mu = jnp.mean(y, axis=-1, keepdims=True)
    yc = y - mu
    var = jnp.mean(yc * yc, axis=-1, keepdims=True)
    t = yc * lax.rsqrt(var + LN_EPS) * lng_ref[...] + lnb_ref[...]
    t = _silu(t) * sz_ref[...].astype(F32)
    out = x_ref[...] + jnp.dot(t.astype(BF16), w_ref[...], preferred_element_type=F32) + bo_ref[...]
    if final_norm:
        out = _rmsnorm_rows(out, fg_ref[...])
    o_ref[...] = out


def _conv_out(u, sz, dw, dw_b, ln_g, ln_b, w, b_out, x, final_gain, *, seq, ts, final_norm):
    m, c = u.shape
    d = w.shape[1]
    tiles_per_seq = seq // ts
    hb = ts // HALO
    last_halo_block = m // HALO - 1
    row = lambda width: pl.BlockSpec((1, width), lambda i: (0, 0))
    vmem = (2 * (ts * c * 4 + 2 * HALO * c * 4 + ts * c * 2 + CONV_KERNEL * c * 4 + c * d * 2
                 + 2 * ts * d * 4) + (ts + 2 * HALO) * c * 4 + ts * c * 4 + 4 * ts * c * 4 + (2 << 20))
    return pl.pallas_call(
        functools.partial(_conv_out_kernel, ts=ts, tiles_per_seq=tiles_per_seq, final_norm=final_norm),
        grid=(m // ts,),
        in_specs=[
            pl.BlockSpec((ts, c), lambda i: (i, 0)),
            pl.BlockSpec((HALO, c), lambda i: (jnp.maximum(i * hb - 1, 0), 0)),
            pl.BlockSpec((HALO, c), lambda i: (jnp.minimum((i + 1) * hb, last_halo_block), 0)),
            pl.BlockSpec((ts, c), lambda i: (i, 0)),
            pl.BlockSpec((CONV_KERNEL, c), lambda i: (0, 0)),
            row(c), row(c), row(c),
            pl.BlockSpec((c, d), lambda i: (0, 0)),
            row(d),
            pl.BlockSpec((ts, d), lambda i: (i, 0)),
            row(d),
        ],
        out_specs=pl.BlockSpec((ts, d), lambda i: (i, 0)),
        out_shape=jax.ShapeDtypeStruct((m, d), F32),
        scratch_shapes=[pltpu.VMEM((ts + 2 * HALO, c), F32), pltpu.VMEM((ts, c), F32)],
        compiler_params=_params(vmem, 1),
        name="conv_out",
    )(u, u, u, sz, dw, dw_b, ln_g, ln_b, w, b_out, x, final_gain)


def kernel(x, norm_gain, final_gain, rel_bias, attn_w_in, attn_w_out, attn_lq1, attn_lk1, attn_lq2,
           attn_lk2, attn_head_gain, conv_w_in, conv_b_in, conv_dw, conv_dw_b, conv_ln_g, conv_ln_b,
           conv_w_out, conv_b_out):
    batch, seq, d = x.shape
    m = batch * seq
    xf = x.reshape(m, d)
    tab = rel_bias.T
    nb = _bias_band(tab)
    row = lambda v: v.reshape(1, -1)
    for layer in range(DEPTH):
        j = layer // 2
        gain = row(norm_gain[layer])
        if layer % 2 == 0:
            hproj = _rms_proj(xf, gain, attn_w_in[j].astype(BF16), n_scaled=QK_WIDTH,
                              col_scale=HEAD_DIM ** -0.5, tm=1024, tn=1024)
            og = _attention(tab, hproj, nb, attn_head_gain[j], row(attn_lq1[j]), row(attn_lk1[j]),
                            row(attn_lq2[j]), row(attn_lk2[j]), batch=batch, seq=seq,
                            lambda_init=_lambda_init(layer))
            xf = _proj_residual(og.reshape(m, ATTN_WIDTH), attn_w_out[j].astype(BF16), xf, tm=512)
        else:
            u, sz = _rms_glu(xf, gain, conv_w_in[j].astype(BF16), row(conv_b_in[j]), tm=512, tn=512)
            xf = _conv_out(u, sz, conv_dw[j], row(conv_dw_b[j]), row(conv_ln_g[j]), row(conv_ln_b[j]),
                           conv_w_out[j].astype(BF16), row(conv_b_out[j]), xf, row(final_gain),
                           seq=seq, ts=256, final_norm=(layer == DEPTH - 1))
    return xf.reshape(batch, seq, d)
```

```python
import functools
import math

import jax
import jax.numpy as jnp
from jax import lax
from jax.experimental import pallas as pl
from jax.experimental.pallas import tpu as pltpu

F32 = jnp.float32
BF16 = jnp.bfloat16

D_MODEL = 2048
DEPTH = 4
HEADS = 8
HEAD_DIM = 128
V_DIM = 2 * HEAD_DIM
QK_WIDTH = HEADS * 2 * HEAD_DIM
ATTN_WIDTH = HEADS * V_DIM
CONV_WIDTH = D_MODEL
CONV_KERNEL = 31
CONV_HALF = CONV_KERNEL // 2
REL_BUCKETS = 32
EPS = 1e-6
LN_EPS = 1e-5
LOG2E = math.log2(math.e)

V7X_LANES = 128
V7X_SUBLANES = 8
VMEM_CAP_BYTES = 56 * 1024 * 1024

TQ = 256
BAND = 2 * TQ
BAND_TILE = BAND + TQ
HALO = 2 * V7X_SUBLANES
CONV_ROWS = 64
BIAS_SATURATION = 91
assert TQ // 2 >= BIAS_SATURATION and HALO >= CONV_HALF


def _lambda_init(layer_idx):
    return 0.8 - 0.6 * math.exp(-0.3 * layer_idx)


def _params(vmem_bytes, ndim):
    return pltpu.CompilerParams(
        dimension_semantics=("arbitrary",) * ndim,
        vmem_limit_bytes=int(min(VMEM_CAP_BYTES, vmem_bytes)),
    )


def _rmsnorm_rows(x, gain):
    ms = jnp.mean(x * x, axis=-1, keepdims=True)
    return x * lax.rsqrt(ms + EPS) * gain


def _silu(x):
    return x * jax.nn.sigmoid(x)


def _rmsnorm_kernel(x_ref, g_ref, o_ref):
    o_ref[...] = _rmsnorm_rows(x_ref[...], g_ref[...]).astype(o_ref.dtype)


def _rmsnorm(x, gain, *, tm):
    m, d = x.shape
    return pl.pallas_call(
        _rmsnorm_kernel,
        grid=(m // tm,),
        in_specs=[pl.BlockSpec((tm, d), lambda i: (i, 0)), pl.BlockSpec((1, d), lambda i: (0, 0))],
        out_specs=pl.BlockSpec((tm, d), lambda i: (i, 0)),
        out_shape=jax.ShapeDtypeStruct((m, d), BF16),
        compiler_params=_params(2 * tm * d * (4 + 2) + 3 * tm * d * 4 + (2 << 20), 1),
        name="rmsnorm",
    )(x, gain)


def _bucket(rel):
    n = jnp.abs(rel)
    large = jnp.full(rel.shape, 8, jnp.int32)
    for thr in (12, 16, 23, 32, 46, 64, BIAS_SATURATION):
        large = large + (n >= thr).astype(jnp.int32)
    return jnp.where(rel > 0, 16, 0) + jnp.where(n < 8, n, large)


def _bias_band_kernel(tab_ref, o_ref):
    h = pl.program_id(0)
    ql = lax.broadcasted_iota(jnp.int32, (TQ, BAND_TILE), 0)
    c = lax.broadcasted_iota(jnp.int32, (TQ, BAND_TILE), 1)
    idx = _bucket(c - TQ - ql)
    acc = jnp.full((TQ, BAND_TILE), tab_ref[h, 0], F32)
    for b in range(1, REL_BUCKETS):
        acc = jnp.where(idx == b, tab_ref[h, b], acc)
    o_ref[...] = acc * LOG2E


def _bias_band(tab):
    return pl.pallas_call(
        _bias_band_kernel,
        grid=(HEADS,),
        in_specs=[pl.BlockSpec(memory_space=pltpu.SMEM)],
        out_specs=pl.BlockSpec((None, TQ, BAND_TILE), lambda h: (h, 0, 0)),
        out_shape=jax.ShapeDtypeStruct((HEADS, TQ, BAND_TILE), F32),
        compiler_params=_params(8 * TQ * BAND_TILE * 4, 1),
        name="bias_band",
    )(tab)


def _proj_kernel(xn_ref, w_ref, o_ref, wb_ref, *, n_scaled_tiles, col_scale):
    jn = pl.program_id(0)

    @pl.when(pl.program_id(1) == 0)
    def _():
        wb_ref[...] = w_ref[...].astype(BF16)

    acc = jnp.dot(xn_ref[...], wb_ref[...], preferred_element_type=F32)
    scale = jnp.where(jn < n_scaled_tiles, col_scale, 1.0).astype(F32)
    o_ref[...] = (acc * scale).astype(o_ref.dtype)


def _proj(xn, w_all, layer, *, n_scaled, col_scale, tm, tn):
    m, d = xn.shape
    n = w_all.shape[2]
    vmem = 2 * d * tn * 4 + d * tn * 2 + 2 * tm * d * 2 + 2 * tm * tn * 2 + 2 * tm * tn * 4 + (2 << 20)
    return pl.pallas_call(
        functools.partial(_proj_kernel, n_scaled_tiles=n_scaled // tn, col_scale=col_scale),
        grid=(n // tn, m // tm),
        in_specs=[
            pl.BlockSpec((tm, d), lambda jn, i: (i, 0)),
            pl.BlockSpec((None, d, tn), lambda jn, i: (layer, 0, jn)),
        ],
        out_specs=pl.BlockSpec((tm, tn), lambda jn, i: (i, jn)),
        out_shape=jax.ShapeDtypeStruct((m, n), BF16),
        scratch_shapes=[pltpu.VMEM((d, tn), BF16)],
        compiler_params=_params(vmem, 2),
        name="attn_in_proj",
    )(xn, w_all)


def _attn_kernel(tab_ref, q_ref, k_ref, v_ref, gate_ref, nb_ref, hg_ref, lq1_ref, lk1_ref, lq2_ref,
                 lk2_ref, o_ref, kaug_ref, sa1, sa2, sb1, sb2, pa1, pa2, pb1, pb2, la1, la2, lb1, lb2,
                 *, seq, n_pairs, pairs_per_head, lambda_init):
    t = pl.program_id(0)
    n_chunks = seq // V7X_LANES
    assert 2 * n_chunks <= V7X_LANES
    dn = (((1,), (1,)), ((), ()))

    @pl.when(t == 0)
    def _():
        for ref in (sb1, sb2):
            ref[...] = jnp.zeros(ref.shape, ref.dtype)
        for ref in (pa1, pa2):
            ref[...] = jnp.zeros(ref.shape, ref.dtype)
        for ref in (la1, la2):
            ref[...] = jnp.ones(ref.shape, ref.dtype)

    pq = jnp.minimum(t, n_pairs - 1)
    head_q = (pq // pairs_per_head) % HEADS
    tile0 = 2 * (pq % pairs_per_head)

    @pl.when((t < n_pairs) & (t % pairs_per_head == 0))
    def _():
        krow = lax.broadcasted_iota(jnp.int32, (seq, V7X_LANES), 0)
        lane = lax.broadcasted_iota(jnp.int32, (seq, V7X_LANES), 1)
        onehot = ((lane < 2 * n_chunks) & (lane % n_chunks == krow // V7X_LANES))
        onehot = jnp.where(onehot, 1.0, 0.0).astype(BF16)
        for c in range(2):
            kaug_ref[c, :, 0:HEAD_DIM] = k_ref[:, c * HEAD_DIM:(c + 1) * HEAD_DIM]
            kaug_ref[c, :, HEAD_DIM:2 * HEAD_DIM] = onehot

    def scores(tile, rows, s_refs):
        q_start = tile * TQ
        ws = jnp.clip(q_start - TQ // 2, 0, seq - BAND)
        boff = pl.multiple_of(ws - q_start + TQ, V7X_LANES)
        ws = pl.multiple_of(ws, V7X_LANES)
        lane = lax.broadcasted_iota(jnp.int32, (1, V7X_LANES), 1)
        chunk = lane % n_chunks
        first = ws // V7X_LANES
        far = jnp.where(chunk < first, tab_ref[head_q, REL_BUCKETS // 2 - 1],
                        jnp.where(chunk >= first + BAND // V7X_LANES, tab_ref[head_q, REL_BUCKETS - 1], 0.0))
        far = far.astype(F32) * LOG2E
        hi = far.astype(BF16).astype(F32)
        feat = jnp.where(lane < n_chunks, hi, jnp.where(lane < 2 * n_chunks, far - hi, 0.0)).astype(BF16)
        feat = jnp.broadcast_to(feat, (TQ, V7X_LANES))
        band = nb_ref[:, pl.ds(boff, BAND)]
        for c, s_ref in enumerate(s_refs):
            qa = jnp.concatenate([q_ref[rows, c * HEAD_DIM:(c + 1) * HEAD_DIM], feat], axis=1)
            s_ref[...] = lax.dot_general(qa, kaug_ref[c], dn, preferred_element_type=F32)
            s_ref[:, pl.ds(ws, BAND)] += band

    def numerators(s_refs, p_refs, l_refs):
        for s_ref, p_ref, l_ref in zip(s_refs, p_refs, l_refs):
            m = jnp.max(s_ref[...], axis=-1, keepdims=True)
            p = jnp.exp2(s_ref[...] - m)
            l_ref[...] = jnp.sum(p, axis=-1, keepdims=True)
            p_ref[...] = p.astype(BF16)

    lam = (jnp.exp(jnp.sum(lq1_ref[...] * lk1_ref[...], axis=-1, keepdims=True))
           - jnp.exp(jnp.sum(lq2_ref[...] * lk2_ref[...], axis=-1, keepdims=True)) + lambda_init)

    def values(p_refs, l_refs, rows):
        v = v_ref[...]
        o1 = jnp.dot(p_refs[0][...], v, preferred_element_type=F32)
        o2 = jnp.dot(p_refs[1][...], v, preferred_element_type=F32)
        o = o1 * (1.0 / l_refs[0][...]) - o2 * (lam / l_refs[1][...])
        o = o * lax.rsqrt(jnp.mean(o * o, axis=-1, keepdims=True) + EPS) * hg_ref[...] * (1.0 - lambda_init)
        o_ref[rows, :] = (o * _silu(gate_ref[rows, :].astype(F32))).astype(o_ref.dtype)

    first_rows, second_rows = pl.ds(0, TQ), pl.ds(TQ, TQ)
    values((pa1, pa2), (la1, la2), first_rows)
    numerators((sb1, sb2), (pb1, pb2), (lb1, lb2))
    scores(tile0, first_rows, (sa1, sa2))
    values((pb1, pb2), (lb1, lb2), second_rows)
    numerators((sa1, sa2), (pa1, pa2), (la1, la2))
    scores(tile0 + 1, second_rows, (sb1, sb2))


def _attention(tab, hproj, nb, head_gain, lq1, lk1, lq2, lk2, *, batch, seq, lambda_init):
    h3 = hproj.reshape(batch, seq, hproj.shape[-1])
    kb, vb, gb = QK_WIDTH // V_DIM, 2 * QK_WIDTH // V_DIM, (2 * QK_WIDTH + ATTN_WIDTH) // V_DIM
    pairs_per_head = seq // (2 * TQ)
    n_pairs = batch * HEADS * pairs_per_head

    def coords(p):
        return p // (HEADS * pairs_per_head), (p // pairs_per_head) % HEADS, p % pairs_per_head

    def score_pair(t):
        return coords(jnp.minimum(t, n_pairs - 1))

    def value_pair(t):
        return coords(jnp.maximum(t - 1, 0))

    def q_map(t):
        b, h, ip = score_pair(t)
        return b, ip, h

    def k_map(t):
        b, h, _ = score_pair(t)
        return b, 0, kb + h

    def v_map(t):
        b, h, _ = value_pair(t)
        return b, 0, vb + h

    def gate_map(t):
        b, h, ip = value_pair(t)
        return b, ip, gb + h

    def out_map(t):
        b, h, ip = value_pair(t)
        return b, ip, h

    vec = pl.BlockSpec((1, HEAD_DIM), lambda t: (0, 0))
    s_buf = pltpu.VMEM((TQ, seq), F32)
    p_buf = pltpu.VMEM((TQ, seq), BF16)
    l_buf = pltpu.VMEM((TQ, 1), F32)
    vmem = (2 * (3 * 2 * TQ * V_DIM * 2 + 2 * seq * V_DIM * 2 + TQ * BAND_TILE * 4)
            + 2 * seq * V_DIM * 2 + 4 * TQ * seq * (4 + 2) + 4 * TQ * V7X_LANES * 4
            + 6 * TQ * seq * 4 + (2 << 20))
    return pl.pallas_call(
        functools.partial(_attn_kernel, seq=seq, n_pairs=n_pairs, pairs_per_head=pairs_per_head,
                          lambda_init=lambda_init),
        grid=(n_pairs + 1,),
        in_specs=[
            pl.BlockSpec(memory_space=pltpu.SMEM),
            pl.BlockSpec((None, 2 * TQ, V_DIM), q_map),
            pl.BlockSpec((None, seq, V_DIM), k_map),
            pl.BlockSpec((None, seq, V_DIM), v_map),
            pl.BlockSpec((None, 2 * TQ, V_DIM), gate_map),
            pl.BlockSpec((None, TQ, BAND_TILE), lambda t: (score_pair(t)[1], 0, 0)),
            pl.BlockSpec((None, 1, V_DIM), lambda t: (value_pair(t)[1], 0, 0)),
            vec, vec, vec, vec,
        ],
        out_specs=pl.BlockSpec((None, 2 * TQ, V_DIM), out_map),
        out_shape=jax.ShapeDtypeStruct((batch, seq, ATTN_WIDTH), BF16),
        scratch_shapes=[pltpu.VMEM((2, seq, V_DIM), BF16)] + [s_buf] * 4 + [p_buf] * 4 + [l_buf] * 4,
        compiler_params=_params(vmem, 1),
        name="diff_attention",
    )(tab, h3, h3, h3, h3, nb, head_gain.reshape(HEADS, 1, V_DIM), lq1, lk1, lq2, lk2)


def _proj_residual_kernel(y_ref, w_ref, x_ref, g_ref, o_ref, xn_ref):
    out = x_ref[...] + jnp.dot(y_ref[...], w_ref[...], preferred_element_type=F32)
    o_ref[...] = out
    xn_ref[...] = _rmsnorm_rows(out, g_ref[...]).astype(xn_ref.dtype)


def _proj_residual(y, w_all, layer, x, next_gain, *, tm):
    m, k = y.shape
    n = w_all.shape[2]
    vmem = 2 * (tm * k * 2 + k * n * 2 + 2 * tm * n * 4 + tm * n * 2) + 3 * tm * n * 4 + (2 << 20)
    return pl.pallas_call(
        _proj_residual_kernel,
        grid=(m // tm,),
        in_specs=[
            pl.BlockSpec((tm, k), lambda i: (i, 0)),
            pl.BlockSpec((None, k, n), lambda i: (layer, 0, 0)),
            pl.BlockSpec((tm, n), lambda i: (i, 0)),
            pl.BlockSpec((1, n), lambda i: (0, 0)),
        ],
        out_specs=[pl.BlockSpec((tm, n), lambda i: (i, 0))] * 2,
        out_shape=[jax.ShapeDtypeStruct((m, n), F32), jax.ShapeDtypeStruct((m, n), BF16)],
        compiler_params=_params(vmem, 1),
        name="attn_out_proj",
    )(y, w_all, x, next_gain)


def _glu_kernel(xn_ref, wa_ref, wg_ref, wz_ref, ba_ref, bg_ref, bz_ref, u_ref, sz_ref, wb_ref):
    @pl.when(pl.program_id(1) == 0)
    def _():
        for part, w_ref in enumerate((wa_ref, wg_ref, wz_ref)):
            wb_ref[part] = w_ref[...].astype(BF16)

    xn = xn_ref[...]
    a = jnp.dot(xn, wb_ref[0], preferred_element_type=F32) + ba_ref[...]
    g = jnp.dot(xn, wb_ref[1], preferred_element_type=F32) + bg_ref[...]
    u_ref[...] = a * jax.nn.sigmoid(g)
    z = jnp.dot(xn, wb_ref[2], preferred_element_type=F32) + bz_ref[...]
    sz_ref[...] = _silu(z).astype(sz_ref.dtype)


def _glu(xn, w_all, b_all, layer, *, tm, tn):
    m, d = xn.shape
    c = w_all.shape[2] // 3
    nj = c // tn
    b3 = b_all.reshape(b_all.shape[0], 1, 3 * c)
    vmem = (2 * 3 * d * tn * 4 + 3 * d * tn * 2 + 2 * tm * d * 2 + 2 * tm * tn * (4 + 2)
            + 4 * tm * tn * 4 + (2 << 20))
    wspec = lambda part: pl.BlockSpec((None, d, tn), lambda jn, i: (layer, 0, part * nj + jn))
    bspec = lambda part: pl.BlockSpec((None, 1, tn), lambda jn, i: (layer, 0, part * nj + jn))
    return pl.pallas_call(
        _glu_kernel,
        grid=(nj, m // tm),
        in_specs=[pl.BlockSpec((tm, d), lambda jn, i: (i, 0)),
                  wspec(0), wspec(1), wspec(2), bspec(0), bspec(1), bspec(2)],
        out_specs=[pl.BlockSpec((tm, tn), lambda jn, i: (i, jn))] * 2,
        out_shape=[jax.ShapeDtypeStruct((m, c), F32), jax.ShapeDtypeStruct((m, c), BF16)],
        scratch_shapes=[pltpu.VMEM((3, d, tn), BF16)],
        compiler_params=_params(vmem, 2),
        name="conv_in_glu",
    )(xn, w_all, w_all, w_all, b3, b3, b3)


def _conv_out_kernel(u_ref, up_ref, un_ref, sz_ref, dw8_ref, dwb_ref, lng_ref, lnb_ref, w_ref, bo_ref,
                     x_ref, g_ref, *refs, ts, tiles_per_seq, last_layer):
    out_refs, (ext_ref, y_ref) = refs[:-2], refs[-2:]
    i = pl.program_id(0)
    pos = i % tiles_per_seq
    c = u_ref.shape[1]
    sub = V7X_SUBLANES

    ext_ref[0:HALO, :] = jnp.where(pos == 0, 0.0, up_ref[...])
    ext_ref[HALO:HALO + ts, :] = u_ref[...]
    ext_ref[HALO + ts:HALO + ts + HALO, :] = jnp.where(pos == tiles_per_seq - 1, 0.0, un_ref[...])

    n_lane_chunks = c // V7X_LANES
    n_out = CONV_ROWS // sub
    first = HALO - CONV_HALF
    n_src = n_out + (first + CONV_KERNEL - 1) // sub + 1
    sub_iota = lax.broadcasted_iota(jnp.int32, (sub, V7X_LANES), 0)

    def conv_block(idx, carry):
        r0 = pl.multiple_of((idx // n_lane_chunks) * CONV_ROWS, sub)
        lanes = pl.ds(pl.multiple_of((idx % n_lane_chunks) * V7X_LANES, V7X_LANES), V7X_LANES)
        src = [ext_ref[pl.ds(r0 + sub * k, sub), lanes] for k in range(n_src)]
        acc = [jnp.broadcast_to(dwb_ref[:, lanes], (sub, V7X_LANES))] * n_out
        for b in range(sub):
            if b == 0:
                shifted = src
            else:
                rot = [pltpu.roll(x, sub - b, axis=0) for x in src]
                shifted = [jnp.where(sub_iota < sub - b, rot[k], rot[k + 1]) for k in range(n_src - 1)]
            for a in range(n_src):
                tap = sub * a + b - first
                if 0 <= tap < CONV_KERNEL:
                    w = dw8_ref[pl.ds(sub * tap, sub), lanes]
                    acc = [acc[v] + shifted[a + v] * w for v in range(n_out)]
        for v in range(n_out):
            y_ref[pl.ds(r0 + sub * v, sub), lanes] = acc[v]
        return carry

    lax.fori_loop(0, (ts // CONV_ROWS) * n_lane_chunks, conv_block, 0)

    y = y_ref[...]
    mu = jnp.mean(y, axis=-1, keepdims=True)
    yc = y - mu
    var = jnp.mean(yc * yc, axis=-1, keepdims=True)
    act = yc * lax.rsqrt(var + LN_EPS) * lng_ref[...] + lnb_ref[...]
    act = _silu(act) * sz_ref[...].astype(F32)
    out = x_ref[...] + jnp.dot(act.astype(BF16), w_ref[...], preferred_element_type=F32) + bo_ref[...]
    if not last_layer:
        out_refs[0][...] = out
    out_refs[-1][...] = _rmsnorm_rows(out, g_ref[...]).astype(out_refs[-1].dtype)


def _conv_out(u, sz, dw, dw_b, ln_g, ln_b, w_all, layer, b_out, x, gain, *, seq, ts, last_layer):
    m, c = u.shape
    d = w_all.shape[2]
    tiles_per_seq = seq // ts
    hb = ts // HALO
    last_halo_block = m // HALO - 1
    dw8 = jnp.broadcast_to(dw[:, None, :], (CONV_KERNEL, V7X_SUBLANES, c)).reshape(CONV_KERNEL * V7X_SUBLANES, c)
    row = lambda width: pl.BlockSpec((1, width), lambda i: (0, 0))
    tile = pl.BlockSpec((ts, d), lambda i: (i, 0))
    if last_layer:
        out_specs, out_shape, out_bytes = [tile], [jax.ShapeDtypeStruct((m, d), F32)], ts * d * 4
    else:
        out_specs = [tile, tile]
        out_shape = [jax.ShapeDtypeStruct((m, d), F32), jax.ShapeDtypeStruct((m, d), BF16)]
        out_bytes = ts * d * (4 + 2)
    vmem = (2 * (ts * c * 4 + 2 * HALO * c * 4 + ts * c * 2 + CONV_KERNEL * V7X_SUBLANES * c * 4 + c * d * 2
                 + ts * d * 4 + out_bytes) + (ts + 2 * HALO) * c * 4 + ts * c * 4 + 4 * ts * c * 4 + (2 << 20))
    return pl.pallas_call(
        functools.partial(_conv_out_kernel, ts=ts, tiles_per_seq=tiles_per_seq, last_layer=last_layer),
        grid=(m // ts,),
        in_specs=[
            pl.BlockSpec((ts, c), lambda i: (i, 0)),
            pl.BlockSpec((HALO, c), lambda i: (jnp.maximum(i * hb - 1, 0), 0)),
            pl.BlockSpec((HALO, c), lambda i: (jnp.minimum((i + 1) * hb, last_halo_block), 0)),
            pl.BlockSpec((ts, c), lambda i: (i, 0)),
            pl.BlockSpec((CONV_KERNEL * V7X_SUBLANES, c), lambda i: (0, 0)),
            row(c), row(c), row(c),
            pl.BlockSpec((None, c, d), lambda i: (layer, 0, 0)),
            row(d),
            tile,
            row(d),
        ],
        out_specs=out_specs,
        out_shape=out_shape,
        scratch_shapes=[pltpu.VMEM((ts + 2 * HALO, c), F32), pltpu.VMEM((ts, c), F32)],
        compiler_params=_params(vmem, 1),
        name="conv_out_proj",
    )(u, u, u, sz, dw8, dw_b, ln_g, ln_b, w_all, b_out, x, gain)


def kernel(x, norm_gain, final_gain, rel_bias, attn_w_in, attn_w_out, attn_lq1, attn_lk1, attn_lq2,
           attn_lk2, attn_head_gain, conv_w_in, conv_b_in, conv_dw, conv_dw_b, conv_ln_g, conv_ln_b,
           conv_w_out, conv_b_out):
    batch, seq, d = x.shape
    m = batch * seq
    xf = x.reshape(m, d)
    row = lambda v: v.reshape(1, -1)
    tab = rel_bias.T
    nb = _bias_band(tab)
    attn_w_out_b = attn_w_out.astype(BF16)
    conv_w_out_b = conv_w_out.astype(BF16)
    gains = [row(norm_gain[layer]) for layer in range(DEPTH)] + [row(final_gain)]
    xn = _rmsnorm(xf, gains[0], tm=512)
    for layer in range(DEPTH):
        j = layer // 2
        if layer % 2 == 0:
            hproj = _proj(xn, attn_w_in, j, n_scaled=QK_WIDTH, col_scale=HEAD_DIM ** -0.5 * LOG2E,
                          tm=1024, tn=1024)
            og = _attention(tab, hproj, nb, attn_head_gain[j], row(attn_lq1[j]), row(attn_lk1[j]),
                            row(attn_lq2[j]), row(attn_lk2[j]), batch=batch, seq=seq,
                            lambda_init=_lambda_init(layer))
            xf, xn = _proj_residual(og.reshape(m, ATTN_WIDTH), attn_w_out_b, j, xf, gains[layer + 1], tm=512)
        else:
            u, sz = _glu(xn, conv_w_in, conv_b_in, j, tm=512, tn=512)
            outs = _conv_out(u, sz, conv_dw[j], row(conv_dw_b[j]), row(conv_ln_g[j]), row(conv_ln_b[j]),
                             conv_w_out_b, j, row(conv_b_out[j]), xf, gains[layer + 1],
                             seq=seq, ts=256, last_layer=(layer == DEPTH - 1))
            if layer == DEPTH - 1:
                return outs[0].reshape(batch, seq, d)
            xf, xn = outs
```

```python
import functools
import math

import jax
import jax.numpy as jnp
from jax import lax
from jax.experimental import pallas as pl
from jax.experimental.pallas import tpu as pltpu

F32 = jnp.float32
BF16 = jnp.bfloat16

D_MODEL = 2048
DEPTH = 4
HEADS = 8
HEAD_DIM = 128
V_DIM = 2 * HEAD_DIM
QK_WIDTH = HEADS * 2 * HEAD_DIM
ATTN_WIDTH = HEADS * V_DIM
CONV_WIDTH = D_MODEL
CONV_KERNEL = 31
CONV_HALF = CONV_KERNEL // 2
REL_BUCKETS = 32
EPS = 1e-6
LN_EPS = 1e-5
LOG2E = math.log2(math.e)

V7X_LANES = 128
V7X_SUBLANES = 8
VMEM_CAP_BYTES = 56 * 1024 * 1024

TQ = 256
BAND = 2 * TQ
BAND_TILE = BAND + TQ
HALO = 2 * V7X_SUBLANES
CONV_STRIDE = 4
CONV_GROUPS = 8
BIAS_SATURATION = 91
assert TQ // 2 >= BIAS_SATURATION and HALO >= CONV_HALF


def _lambda_init(layer_idx):
    return 0.8 - 0.6 * math.exp(-0.3 * layer_idx)


def _params(vmem_bytes, ndim):
    return pltpu.CompilerParams(
        dimension_semantics=("arbitrary",) * ndim,
        vmem_limit_bytes=int(min(VMEM_CAP_BYTES, vmem_bytes)),
    )


def _rmsnorm_rows(x, gain):
    ms = jnp.mean(x * x, axis=-1, keepdims=True)
    return x * lax.rsqrt(ms + EPS) * gain


def _silu(x):
    return x * jax.nn.sigmoid(x)


def _rmsnorm_kernel(x_ref, g_ref, o_ref):
    o_ref[...] = _rmsnorm_rows(x_ref[...], g_ref[...]).astype(o_ref.dtype)


def _rmsnorm(x, gain, *, tm):
    m, d = x.shape
    return pl.pallas_call(
        _rmsnorm_kernel,
        grid=(m // tm,),
        in_specs=[pl.BlockSpec((tm, d), lambda i: (i, 0)), pl.BlockSpec((1, d), lambda i: (0, 0))],
        out_specs=pl.BlockSpec((tm, d), lambda i: (i, 0)),
        out_shape=jax.ShapeDtypeStruct((m, d), BF16),
        compiler_params=_params(2 * tm * d * (4 + 2) + 3 * tm * d * 4 + (2 << 20), 1),
        name="rmsnorm",
    )(x, gain)


def _bucket(rel):
    n = jnp.abs(rel)
    large = jnp.full(rel.shape, 8, jnp.int32)
    for thr in (12, 16, 23, 32, 46, 64, BIAS_SATURATION):
        large = large + (n >= thr).astype(jnp.int32)
    return jnp.where(rel > 0, 16, 0) + jnp.where(n < 8, n, large)


def _bias_band_kernel(tab_ref, o_ref):
    h = pl.program_id(0)
    ql = lax.broadcasted_iota(jnp.int32, (TQ, BAND_TILE), 0)
    c = lax.broadcasted_iota(jnp.int32, (TQ, BAND_TILE), 1)
    idx = _bucket(c - TQ - ql)
    acc = jnp.full((TQ, BAND_TILE), tab_ref[h, 0], F32)
    for b in range(1, REL_BUCKETS):
        acc = jnp.where(idx == b, tab_ref[h, b], acc)
    o_ref[...] = acc * LOG2E


def _bias_band(tab):
    return pl.pallas_call(
        _bias_band_kernel,
        grid=(HEADS,),
        in_specs=[pl.BlockSpec(memory_space=pltpu.SMEM)],
        out_specs=pl.BlockSpec((None, TQ, BAND_TILE), lambda h: (h, 0, 0)),
        out_shape=jax.ShapeDtypeStruct((HEADS, TQ, BAND_TILE), F32),
        compiler_params=_params(8 * TQ * BAND_TILE * 4, 1),
        name="bias_band",
    )(tab)


def _proj_kernel(xn_ref, w_ref, o_ref, wb_ref, *, n_scaled_tiles, col_scale):
    jn = pl.program_id(0)

    @pl.when(pl.program_id(1) == 0)
    def _():
        wb_ref[...] = w_ref[...].astype(BF16)

    acc = jnp.dot(xn_ref[...], wb_ref[...], preferred_element_type=F32)
    scale = jnp.where(jn < n_scaled_tiles, col_scale, 1.0).astype(F32)
    o_ref[...] = (acc * scale).astype(o_ref.dtype)


def _proj(xn, w_all, layer, *, n_scaled, col_scale, tm, tn):
    m, d = xn.shape
    n = w_all.shape[2]
    vmem = 2 * d * tn * 4 + d * tn * 2 + 2 * tm * d * 2 + 2 * tm * tn * 2 + 2 * tm * tn * 4 + (2 << 20)
    return pl.pallas_call(
        functools.partial(_proj_kernel, n_scaled_tiles=n_scaled // tn, col_scale=col_scale),
        grid=(n // tn, m // tm),
        in_specs=[
            pl.BlockSpec((tm, d), lambda jn, i: (i, 0)),
            pl.BlockSpec((None, d, tn), lambda jn, i: (layer, 0, jn)),
        ],
        out_specs=pl.BlockSpec((tm, tn), lambda jn, i: (i, jn)),
        out_shape=jax.ShapeDtypeStruct((m, n), BF16),
        scratch_shapes=[pltpu.VMEM((d, tn), BF16)],
        compiler_params=_params(vmem, 2),
        name="attn_in_proj",
    )(xn, w_all)


def _attn_kernel(tab_ref, q_ref, k_ref, v_ref, gate_ref, nb_ref, hg_ref, lq1_ref, lk1_ref, lq2_ref,
                 lk2_ref, o_ref, kaug_ref, sa1, sa2, sb1, sb2, pa1, pa2, pb1, pb2, la1, la2, lb1, lb2,
                 *, seq, n_pairs, pairs_per_head, lambda_init):
    t = pl.program_id(0)
    n_chunks = seq // V7X_LANES
    assert 2 * n_chunks <= V7X_LANES

    @pl.when(t == 0)
    def _():
        for ref in (sb1, sb2):
            ref[...] = jnp.zeros(ref.shape, ref.dtype)
        for ref in (pa1, pa2):
            ref[...] = jnp.zeros(ref.shape, ref.dtype)
        for ref in (la1, la2):
            ref[...] = jnp.ones(ref.shape, ref.dtype)

    pq = jnp.minimum(t, n_pairs - 1)
    head_q = (pq // pairs_per_head) % HEADS
    tile0 = 2 * (pq % pairs_per_head)

    @pl.when((t < n_pairs) & (t % pairs_per_head == 0))
    def _():
        feat_row = lax.broadcasted_iota(jnp.int32, (HEAD_DIM, seq), 0)
        key = lax.broadcasted_iota(jnp.int32, (HEAD_DIM, seq), 1)
        onehot = ((feat_row < 2 * n_chunks) & (feat_row % n_chunks == key // V7X_LANES))
        onehot = jnp.where(onehot, 1.0, 0.0).astype(BF16)
        for c in range(2):
            kaug_ref[c, 0:HEAD_DIM, :] = k_ref[:, c * HEAD_DIM:(c + 1) * HEAD_DIM].T
            kaug_ref[c, HEAD_DIM:2 * HEAD_DIM, :] = onehot

    def scores(tile, rows, s_refs):
        q_start = tile * TQ
        ws = jnp.clip(q_start - TQ // 2, 0, seq - BAND)
        boff = pl.multiple_of(ws - q_start + TQ, V7X_LANES)
        ws = pl.multiple_of(ws, V7X_LANES)
        lane = lax.broadcasted_iota(jnp.int32, (1, V7X_LANES), 1)
        chunk = lane % n_chunks
        first = ws // V7X_LANES
        far = jnp.where(chunk < first, tab_ref[head_q, REL_BUCKETS // 2 - 1],
                        jnp.where(chunk >= first + BAND // V7X_LANES, tab_ref[head_q, REL_BUCKETS - 1], 0.0))
        far = far.astype(F32) * LOG2E
        hi = far.astype(BF16).astype(F32)
        feat = jnp.where(lane < n_chunks, hi, jnp.where(lane < 2 * n_chunks, far - hi, 0.0)).astype(BF16)
        feat = jnp.broadcast_to(feat, (TQ, V7X_LANES))
        band = nb_ref[:, pl.ds(boff, BAND)]
        for c, s_ref in enumerate(s_refs):
            qa = jnp.concatenate([q_ref[rows, c * HEAD_DIM:(c + 1) * HEAD_DIM], feat], axis=1)
            s_ref[...] = jnp.dot(qa, kaug_ref[c], preferred_element_type=F32)
            s_ref[:, pl.ds(ws, BAND)] += band

    def numerators(s_refs, p_refs, l_refs):
        for s_ref, p_ref, l_ref in zip(s_refs, p_refs, l_refs):
            m = jnp.max(s_ref[...], axis=-1, keepdims=True)
            p = jnp.exp2(s_ref[...] - m)
            l_ref[...] = jnp.sum(p, axis=-1, keepdims=True)
            p_ref[...] = p.astype(BF16)

    lam = (jnp.exp(jnp.sum(lq1_ref[...] * lk1_ref[...], axis=-1, keepdims=True))
           - jnp.exp(jnp.sum(lq2_ref[...] * lk2_ref[...], axis=-1, keepdims=True)) + lambda_init)

    def values(p_refs, l_refs, rows):
        v = v_ref[...]
        o1 = jnp.dot(p_refs[0][...], v, preferred_element_type=F32)
        o2 = jnp.dot(p_refs[1][...], v, preferred_element_type=F32)
        o = o1 * (1.0 / l_refs[0][...]) - o2 * (lam / l_refs[1][...])
        o = o * lax.rsqrt(jnp.mean(o * o, axis=-1, keepdims=True) + EPS) * hg_ref[...] * (1.0 - lambda_init)
        o_ref[rows, :] = (o * _silu(gate_ref[rows, :].astype(F32))).astype(o_ref.dtype)

    first_rows, second_rows = pl.ds(0, TQ), pl.ds(TQ, TQ)
    values((pa1, pa2), (la1, la2), first_rows)
    numerators((sb1, sb2), (pb1, pb2), (lb1, lb2))
    scores(tile0, first_rows, (sa1, sa2))
    values((pb1, pb2), (lb1, lb2), second_rows)
    numerators((sa1, sa2), (pa1, pa2), (la1, la2))
    scores(tile0 + 1, second_rows, (sb1, sb2))


def _attention(tab, hproj, nb, head_gain, lq1, lk1, lq2, lk2, *, batch, seq, lambda_init):
    h3 = hproj.reshape(batch, seq, hproj.shape[-1])
    kb, vb, gb = QK_WIDTH // V_DIM, 2 * QK_WIDTH // V_DIM, (2 * QK_WIDTH + ATTN_WIDTH) // V_DIM
    pairs_per_head = seq // (2 * TQ)
    n_pairs = batch * HEADS * pairs_per_head

    def coords(p):
        return p // (HEADS * pairs_per_head), (p // pairs_per_head) % HEADS, p % pairs_per_head

    def score_pair(t):
        return coords(jnp.minimum(t, n_pairs - 1))

    def value_pair(t):
        return coords(jnp.maximum(t - 1, 0))

    def q_map(t):
        b, h, ip = score_pair(t)
        return b, ip, h

    def k_map(t):
        b, h, _ = score_pair(t)
        return b, 0, kb + h

    def v_map(t):
        b, h, _ = value_pair(t)
        return b, 0, vb + h

    def gate_map(t):
        b, h, ip = value_pair(t)
        return b, ip, gb + h

    def out_map(t):
        b, h, ip = value_pair(t)
        return b, ip, h

    vec = pl.BlockSpec((1, HEAD_DIM), lambda t: (0, 0))
    s_buf = pltpu.VMEM((TQ, seq), F32)
    p_buf = pltpu.VMEM((TQ, seq), BF16)
    l_buf = pltpu.VMEM((TQ, 1), F32)
    vmem = (2 * (3 * 2 * TQ * V_DIM * 2 + 2 * seq * V_DIM * 2 + TQ * BAND_TILE * 4)
            + 2 * seq * V_DIM * 2 + 4 * TQ * seq * (4 + 2) + 4 * TQ * V7X_LANES * 4
            + 6 * TQ * seq * 4 + (2 << 20))
    return pl.pallas_call(
        functools.partial(_attn_kernel, seq=seq, n_pairs=n_pairs, pairs_per_head=pairs_per_head,
                          lambda_init=lambda_init),
        grid=(n_pairs + 1,),
        in_specs=[
            pl.BlockSpec(memory_space=pltpu.SMEM),
            pl.BlockSpec((None, 2 * TQ, V_DIM), q_map),
            pl.BlockSpec((None, seq, V_DIM), k_map),
            pl.BlockSpec((None, seq, V_DIM), v_map),
            pl.BlockSpec((None, 2 * TQ, V_DIM), gate_map),
            pl.BlockSpec((None, TQ, BAND_TILE), lambda t: (score_pair(t)[1], 0, 0)),
            pl.BlockSpec((None, 1, V_DIM), lambda t: (value_pair(t)[1], 0, 0)),
            vec, vec, vec, vec,
        ],
        out_specs=pl.BlockSpec((None, 2 * TQ, V_DIM), out_map),
        out_shape=jax.ShapeDtypeStruct((batch, seq, ATTN_WIDTH), BF16),
        scratch_shapes=[pltpu.VMEM((2, V_DIM, seq), BF16)] + [s_buf] * 4 + [p_buf] * 4 + [l_buf] * 4,
        compiler_params=_params(vmem, 1),
        name="diff_attention",
    )(tab, h3, h3, h3, h3, nb, head_gain.reshape(HEADS, 1, V_DIM), lq1, lk1, lq2, lk2)


def _proj_residual_kernel(y_ref, w_ref, x_ref, g_ref, o_ref, xn_ref):
    out = x_ref[...] + jnp.dot(y_ref[...], w_ref[...], preferred_element_type=F32)
    o_ref[...] = out
    xn_ref[...] = _rmsnorm_rows(out, g_ref[...]).astype(xn_ref.dtype)


def _proj_residual(y, w_all, layer, x, next_gain, *, tm):
    m, k = y.shape
    n = w_all.shape[2]
    vmem = 2 * (tm * k * 2 + k * n * 2 + 2 * tm * n * 4 + tm * n * 2) + 3 * tm * n * 4 + (2 << 20)
    return pl.pallas_call(
        _proj_residual_kernel,
        grid=(m // tm,),
        in_specs=[
            pl.BlockSpec((tm, k), lambda i: (i, 0)),
            pl.BlockSpec((None, k, n), lambda i: (layer, 0, 0)),
            pl.BlockSpec((tm, n), lambda i: (i, 0)),
            pl.BlockSpec((1, n), lambda i: (0, 0)),
        ],
        out_specs=[pl.BlockSpec((tm, n), lambda i: (i, 0))] * 2,
        out_shape=[jax.ShapeDtypeStruct((m, n), F32), jax.ShapeDtypeStruct((m, n), BF16)],
        compiler_params=_params(vmem, 1),
        name="attn_out_proj",
    )(y, w_all, x, next_gain)


def _glu_kernel(xn_ref, wa_ref, wg_ref, wz_ref, ba_ref, bg_ref, bz_ref, u_ref, sz_ref, wb_ref):
    @pl.when(pl.program_id(1) == 0)
    def _():
        for part, w_ref in enumerate((wa_ref, wg_ref, wz_ref)):
            wb_ref[part] = w_ref[...].astype(BF16)

    xn = xn_ref[...]
    a = jnp.dot(xn, wb_ref[0], preferred_element_type=F32) + ba_ref[...]
    g = jnp.dot(xn, wb_ref[1], preferred_element_type=F32) + bg_ref[...]
    u_ref[...] = a * jax.nn.sigmoid(g)
    z = jnp.dot(xn, wb_ref[2], preferred_element_type=F32) + bz_ref[...]
    sz_ref[...] = _silu(z).astype(sz_ref.dtype)


def _glu(xn, w_all, b_all, layer, *, tm, tn):
    m, d = xn.shape
    c = w_all.shape[2] // 3
    nj = c // tn
    b3 = b_all.reshape(b_all.shape[0], 1, 3 * c)
    vmem = (2 * 3 * d * tn * 4 + 3 * d * tn * 2 + 2 * tm * d * 2 + 2 * tm * tn * (4 + 2)
            + 4 * tm * tn * 4 + (2 << 20))
    wspec = lambda part: pl.BlockSpec((None, d, tn), lambda jn, i: (layer, 0, part * nj + jn))
    bspec = lambda part: pl.BlockSpec((None, 1, tn), lambda jn, i: (layer, 0, part * nj + jn))
    return pl.pallas_call(
        _glu_kernel,
        grid=(nj, m // tm),
        in_specs=[pl.BlockSpec((tm, d), lambda jn, i: (i, 0)),
                  wspec(0), wspec(1), wspec(2), bspec(0), bspec(1), bspec(2)],
        out_specs=[pl.BlockSpec((tm, tn), lambda jn, i: (i, jn))] * 2,
        out_shape=[jax.ShapeDtypeStruct((m, c), F32), jax.ShapeDtypeStruct((m, c), BF16)],
        scratch_shapes=[pltpu.VMEM((3, d, tn), BF16)],
        compiler_params=_params(vmem, 2),
        name="conv_in_glu",
    )(xn, w_all, w_all, w_all, b3, b3, b3)


def _conv_out_kernel(u_ref, up_ref, un_ref, sz_ref, dw8_ref, dwb_ref, lng_ref, lnb_ref, w_ref, bo_ref,
                     x_ref, g_ref, *refs, ts, tiles_per_seq, last_layer):
    out_refs, (ext_ref, y_ref) = refs[:-2], refs[-2:]
    i = pl.program_id(0)
    pos = i % tiles_per_seq
    c = u_ref.shape[1]
    sub = V7X_SUBLANES

    n_lane_chunks = c // V7X_LANES
    head = jnp.where(pos == 0, 0.0, up_ref[...])
    tail = jnp.where(pos == tiles_per_seq - 1, 0.0, un_ref[...])
    for cb in range(n_lane_chunks):
        lanes = slice(cb * V7X_LANES, (cb + 1) * V7X_LANES)
        ext_ref[cb, 0:HALO, :] = head[:, lanes]
        ext_ref[cb, HALO:HALO + ts, :] = u_ref[:, lanes]
        ext_ref[cb, HALO + ts:HALO + ts + HALO, :] = tail[:, lanes]

    first = HALO - CONV_HALF
    group = sub * CONV_STRIDE

    def conv_block(idx, carry):
        cb = idx % n_lane_chunks
        lanes = pl.ds(pl.multiple_of(cb * V7X_LANES, V7X_LANES), V7X_LANES)
        bias = jnp.broadcast_to(dwb_ref[:, lanes], (sub, V7X_LANES))
        for g in range(CONV_GROUPS):
            g0 = ((idx // n_lane_chunks) * CONV_GROUPS + g) * group
            src = [ext_ref[cb, pl.ds(g0 + first + k, sub, stride=CONV_STRIDE), :]
                   for k in range(CONV_STRIDE + CONV_KERNEL - 1)]
            acc = [[bias] * CONV_STRIDE, [None] * CONV_STRIDE]
            for tap in range(CONV_KERNEL):
                w = dw8_ref[pl.ds(sub * tap, sub), lanes]
                part = acc[tap % 2]
                for j in range(CONV_STRIDE):
                    prod = src[j + tap] * w
                    part[j] = prod if part[j] is None else part[j] + prod
            for j in range(CONV_STRIDE):
                y_ref[cb, pl.ds(g0 + j, sub, stride=CONV_STRIDE), :] = acc[0][j] + acc[1][j]
        return carry

    lax.fori_loop(0, (ts // (group * CONV_GROUPS)) * n_lane_chunks, conv_block, 0)

    y = jnp.concatenate([y_ref[cb] for cb in range(n_lane_chunks)], axis=1)
    mu = jnp.mean(y, axis=-1, keepdims=True)
    yc = y - mu
    var = jnp.mean(yc * yc, axis=-1, keepdims=True)
    act = yc * lax.rsqrt(var + LN_EPS) * lng_ref[...] + lnb_ref[...]
    act = _silu(act) * sz_ref[...].astype(F32)
    out = x_ref[...] + jnp.dot(act.astype(BF16), w_ref[...], preferred_element_type=F32) + bo_ref[...]
    if not last_layer:
        out_refs[0][...] = out
    out_refs[-1][...] = _rmsnorm_rows(out, g_ref[...]).astype(out_refs[-1].dtype)


def _conv_out(u, sz, dw, dw_b, ln_g, ln_b, w_all, layer, b_out, x, gain, *, seq, ts, last_layer):
    m, c = u.shape
    d = w_all.shape[2]
    tiles_per_seq = seq // ts
    hb = ts // HALO
    last_halo_block = m // HALO - 1
    dw8 = jnp.broadcast_to(dw[:, None, :], (CONV_KERNEL, V7X_SUBLANES, c)).reshape(CONV_KERNEL * V7X_SUBLANES, c)
    row = lambda width: pl.BlockSpec((1, width), lambda i: (0, 0))
    tile = pl.BlockSpec((ts, d), lambda i: (i, 0))
    if last_layer:
        out_specs, out_shape, out_bytes = [tile], [jax.ShapeDtypeStruct((m, d), F32)], ts * d * 4
    else:
        out_specs = [tile, tile]
        out_shape = [jax.ShapeDtypeStruct((m, d), F32), jax.ShapeDtypeStruct((m, d), BF16)]
        out_bytes = ts * d * (4 + 2)
    vmem = (2 * (ts * c * 4 + 2 * HALO * c * 4 + ts * c * 2 + CONV_KERNEL * V7X_SUBLANES * c * 4 + c * d * 2
                 + ts * d * 4 + out_bytes) + (ts + 2 * HALO) * c * 4 + ts * c * 4 + 4 * ts * c * 4 + (2 << 20))
    return pl.pallas_call(
        functools.partial(_conv_out_kernel, ts=ts, tiles_per_seq=tiles_per_seq, last_layer=last_layer),
        grid=(m // ts,),
        in_specs=[
            pl.BlockSpec((ts, c), lambda i: (i, 0)),
            pl.BlockSpec((HALO, c), lambda i: (jnp.maximum(i * hb - 1, 0), 0)),
            pl.BlockSpec((HALO, c), lambda i: (jnp.minimum((i + 1) * hb, last_halo_block), 0)),
            pl.BlockSpec((ts, c), lambda i: (i, 0)),
            pl.BlockSpec((CONV_KERNEL * V7X_SUBLANES, c), lambda i: (0, 0)),
            row(c), row(c), row(c),
            pl.BlockSpec((None, c, d), lambda i: (layer, 0, 0)),
            row(d),
            tile,
            row(d),
        ],
        out_specs=out_specs,
        out_shape=out_shape,
        scratch_shapes=[pltpu.VMEM((c // V7X_LANES, ts + 2 * HALO, V7X_LANES), F32),
                        pltpu.VMEM((c // V7X_LANES, ts, V7X_LANES), F32)],
        compiler_params=_params(vmem, 1),
        name="conv_out_proj",
    )(u, u, u, sz, dw8, dw_b, ln_g, ln_b, w_all, b_out, x, gain)


def kernel(x, norm_gain, final_gain, rel_bias, attn_w_in, attn_w_out, attn_lq1, attn_lk1, attn_lq2,
           attn_lk2, attn_head_gain, conv_w_in, conv_b_in, conv_dw, conv_dw_b, conv_ln_g, conv_ln_b,
           conv_w_out, conv_b_out):
    batch, seq, d = x.shape
    m = batch * seq
    xf = x.reshape(m, d)
    row = lambda v: v.reshape(1, -1)
    tab = rel_bias.T
    nb = _bias_band(tab)
    attn_w_out_b = attn_w_out.astype(BF16)
    conv_w_out_b = conv_w_out.astype(BF16)
    gains = [row(norm_gain[layer]) for layer in range(DEPTH)] + [row(final_gain)]
    xn = _rmsnorm(xf, gains[0], tm=512)
    for layer in range(DEPTH):
        j = layer // 2
        if layer % 2 == 0:
            hproj = _proj(xn, attn_w_in, j, n_scaled=QK_WIDTH, col_scale=HEAD_DIM ** -0.5 * LOG2E,
                          tm=1024, tn=1024)
            og = _attention(tab, hproj, nb, attn_head_gain[j], row(attn_lq1[j]), row(attn_lk1[j]),
                            row(attn_lq2[j]), row(attn_lk2[j]), batch=batch, seq=seq,
                            lambda_init=_lambda_init(layer))
            xf, xn = _proj_residual(og.reshape(m, ATTN_WIDTH), attn_w_out_b, j, xf, gains[layer + 1], tm=512)
        else:
            u, sz = _glu(xn, conv_w_in, conv_b_in, j, tm=512, tn=512)
            outs = _conv_out(u, sz, conv_dw[j], row(conv_dw_b[j]), row(conv_ln_g[j]), row(conv_ln_b[j]),
                             conv_w_out_b, j, row(conv_b_out[j]), xf, gains[layer + 1],
                             seq=seq, ts=256, last_layer=(layer == DEPTH - 1))
            if layer == DEPTH - 1:
                return outs[0].reshape(batch, seq, d)
            xf, xn = outs
```

```python
import functools
import math

import jax
import jax.numpy as jnp
from jax import lax
from jax.experimental import pallas as pl
from jax.experimental.pallas import tpu as pltpu

F32 = jnp.float32
BF16 = jnp.bfloat16

D_MODEL = 2048
DEPTH = 4
HEADS = 8
HEAD_DIM = 128
V_DIM = 2 * HEAD_DIM
QK_WIDTH = HEADS * 2 * HEAD_DIM
ATTN_WIDTH = HEADS * V_DIM
CONV_WIDTH = D_MODEL
CONV_KERNEL = 31
CONV_HALF = CONV_KERNEL // 2
REL_BUCKETS = 32
EPS = 1e-6
LN_EPS = 1e-5
LOG2E = math.log2(math.e)

V7X_LANES = 128
V7X_SUBLANES = 8
VMEM_CAP_BYTES = 56 * 1024 * 1024

TQ = 256
BAND = 2 * TQ
BAND_TILE = BAND + TQ
HALO = 2 * V7X_SUBLANES
CONV_STRIDE = 4
CONV_GROUPS = 8
BIAS_SATURATION = 91
assert TQ // 2 >= BIAS_SATURATION and HALO >= CONV_HALF


def _lambda_init(layer_idx):
    return 0.8 - 0.6 * math.exp(-0.3 * layer_idx)


def _params(vmem_bytes, ndim):
    return pltpu.CompilerParams(
        dimension_semantics=("arbitrary",) * ndim,
        vmem_limit_bytes=int(min(VMEM_CAP_BYTES, vmem_bytes)),
    )


def _rmsnorm_rows(x, gain):
    ms = jnp.mean(x * x, axis=-1, keepdims=True)
    return x * lax.rsqrt(ms + EPS) * gain


def _silu(x):
    return x * jax.nn.sigmoid(x)


def _rmsnorm_kernel(x_ref, g_ref, o_ref):
    o_ref[...] = _rmsnorm_rows(x_ref[...], g_ref[...]).astype(o_ref.dtype)


def _rmsnorm(x, gain, *, tm):
    m, d = x.shape
    return pl.pallas_call(
        _rmsnorm_kernel,
        grid=(m // tm,),
        in_specs=[pl.BlockSpec((tm, d), lambda i: (i, 0)), pl.BlockSpec((1, d), lambda i: (0, 0))],
        out_specs=pl.BlockSpec((tm, d), lambda i: (i, 0)),
        out_shape=jax.ShapeDtypeStruct((m, d), BF16),
        compiler_params=_params(2 * tm * d * (4 + 2) + 3 * tm * d * 4 + (2 << 20), 1),
        name="rmsnorm",
    )(x, gain)


def _bucket(rel):
    n = jnp.abs(rel)
    large = jnp.full(rel.shape, 8, jnp.int32)
    for thr in (12, 16, 23, 32, 46, 64, BIAS_SATURATION):
        large = large + (n >= thr).astype(jnp.int32)
    return jnp.where(rel > 0, 16, 0) + jnp.where(n < 8, n, large)


def _bias_band_kernel(tab_ref, o_ref):
    h = pl.program_id(0)
    ql = lax.broadcasted_iota(jnp.int32, (TQ, BAND_TILE), 0)
    c = lax.broadcasted_iota(jnp.int32, (TQ, BAND_TILE), 1)
    idx = _bucket(c - TQ - ql)
    acc = jnp.full((TQ, BAND_TILE), tab_ref[h, 0], F32)
    for b in range(1, REL_BUCKETS):
        acc = jnp.where(idx == b, tab_ref[h, b], acc)
    o_ref[...] = acc * LOG2E


def _bias_band(tab):
    return pl.pallas_call(
        _bias_band_kernel,
        grid=(HEADS,),
        in_specs=[pl.BlockSpec(memory_space=pltpu.SMEM)],
        out_specs=pl.BlockSpec((None, TQ, BAND_TILE), lambda h: (h, 0, 0)),
        out_shape=jax.ShapeDtypeStruct((HEADS, TQ, BAND_TILE), F32),
        compiler_params=_params(8 * TQ * BAND_TILE * 4, 1),
        name="bias_band",
    )(tab)


def _proj_kernel(xn_ref, w_ref, wo_ref, o_ref, wob_ref, wb_ref, *, n_scaled_tiles, col_scale):
    jn = pl.program_id(0)
    wob_ref[...] = wo_ref[...].astype(BF16)

    @pl.when(pl.program_id(1) == 0)
    def _():
        wb_ref[...] = w_ref[...].astype(BF16)

    acc = jnp.dot(xn_ref[...], wb_ref[...], preferred_element_type=F32)
    scale = jnp.where(jn < n_scaled_tiles, col_scale, 1.0).astype(F32)
    o_ref[...] = (acc * scale).astype(o_ref.dtype)


def _cast_slice_specs(wo_all, layer, grid):
    _, rows, cols = wo_all.shape
    n_steps = grid[0] * grid[1]
    step = lambda a, b: a * grid[1] + b
    return (pl.BlockSpec((None, rows // n_steps, cols), lambda a, b: (layer, step(a, b), 0)),
            pl.BlockSpec((rows // n_steps, cols), lambda a, b: (step(a, b), 0)),
            jax.ShapeDtypeStruct((rows, cols), BF16))


def _proj(xn, w_all, wo_all, layer, *, n_scaled, col_scale, tm, tn):
    m, d = xn.shape
    n = w_all.shape[2]
    grid = (n // tn, m // tm)
    wo_in, wo_out, wo_shape = _cast_slice_specs(wo_all, layer, grid)
    vmem = 2 * d * tn * 4 + d * tn * 2 + 2 * tm * d * 2 + 2 * tm * tn * 2 + 2 * tm * tn * 4 + (2 << 20)
    return pl.pallas_call(
        functools.partial(_proj_kernel, n_scaled_tiles=n_scaled // tn, col_scale=col_scale),
        grid=grid,
        in_specs=[
            pl.BlockSpec((tm, d), lambda jn, i: (i, 0)),
            pl.BlockSpec((None, d, tn), lambda jn, i: (layer, 0, jn)),
            wo_in,
        ],
        out_specs=[pl.BlockSpec((tm, tn), lambda jn, i: (i, jn)), wo_out],
        out_shape=[jax.ShapeDtypeStruct((m, n), BF16), wo_shape],
        scratch_shapes=[pltpu.VMEM((d, tn), BF16)],
        compiler_params=_params(vmem, 2),
        name="attn_in_proj",
    )(xn, w_all, wo_all)


def _attn_kernel(tab_ref, q_ref, k_ref, v_ref, gate_ref, nb_ref, hg_ref, lq1_ref, lk1_ref, lq2_ref,
                 lk2_ref, o_ref, kaug_ref, sa1, sa2, sb1, sb2, pa1, pa2, pb1, pb2, la1, la2, lb1, lb2,
                 *, seq, n_pairs, pairs_per_head, lambda_init):
    t = pl.program_id(0)
    n_chunks = seq // V7X_LANES
    assert 2 * n_chunks <= V7X_LANES

    @pl.when(t == 0)
    def _():
        for ref in (sb1, sb2):
            ref[...] = jnp.zeros(ref.shape, ref.dtype)
        for ref in (pa1, pa2):
            ref[...] = jnp.zeros(ref.shape, ref.dtype)
        for ref in (la1, la2):
            ref[...] = jnp.ones(ref.shape, ref.dtype)

    pq = jnp.minimum(t, n_pairs - 1)
    head_q = (pq // pairs_per_head) % HEADS
    tile0 = 2 * (pq % pairs_per_head)

    @pl.when((t < n_pairs) & (t % pairs_per_head == 0))
    def _():
        feat_row = lax.broadcasted_iota(jnp.int32, (HEAD_DIM, seq), 0)
        key = lax.broadcasted_iota(jnp.int32, (HEAD_DIM, seq), 1)
        onehot = ((feat_row < 2 * n_chunks) & (feat_row % n_chunks == key // V7X_LANES))
        onehot = jnp.where(onehot, 1.0, 0.0).astype(BF16)
        for c in range(2):
            kaug_ref[c, 0:HEAD_DIM, :] = k_ref[:, c * HEAD_DIM:(c + 1) * HEAD_DIM].T
            kaug_ref[c, HEAD_DIM:2 * HEAD_DIM, :] = onehot

    def scores(tile, rows, s_refs):
        q_start = tile * TQ
        ws = jnp.clip(q_start - TQ // 2, 0, seq - BAND)
        boff = pl.multiple_of(ws - q_start + TQ, V7X_LANES)
        ws = pl.multiple_of(ws, V7X_LANES)
        lane = lax.broadcasted_iota(jnp.int32, (1, V7X_LANES), 1)
        chunk = lane % n_chunks
        first = ws // V7X_LANES
        far = jnp.where(chunk < first, tab_ref[head_q, REL_BUCKETS // 2 - 1],
                        jnp.where(chunk >= first + BAND // V7X_LANES, tab_ref[head_q, REL_BUCKETS - 1], 0.0))
        far = far.astype(F32) * LOG2E
        hi = far.astype(BF16).astype(F32)
        feat = jnp.where(lane < n_chunks, hi, jnp.where(lane < 2 * n_chunks, far - hi, 0.0)).astype(BF16)
        feat = jnp.broadcast_to(feat, (TQ, V7X_LANES))
        band = nb_ref[:, pl.ds(boff, BAND)]
        for c, s_ref in enumerate(s_refs):
            qa = jnp.concatenate([q_ref[rows, c * HEAD_DIM:(c + 1) * HEAD_DIM], feat], axis=1)
            s_ref[...] = jnp.dot(qa, kaug_ref[c], preferred_element_type=F32)
            s_ref[:, pl.ds(ws, BAND)] += band

    def numerators(s_refs, p_refs, l_refs):
        for s_ref, p_ref, l_ref in zip(s_refs, p_refs, l_refs):
            m = jnp.max(s_ref[...], axis=-1, keepdims=True)
            p = jnp.exp2(s_ref[...] - m)
            l_ref[...] = jnp.sum(p, axis=-1, keepdims=True)
            p_ref[...] = p.astype(BF16)

    lam = (jnp.exp(jnp.sum(lq1_ref[...] * lk1_ref[...], axis=-1, keepdims=True))
           - jnp.exp(jnp.sum(lq2_ref[...] * lk2_ref[...], axis=-1, keepdims=True)) + lambda_init)

    def values(p_refs, l_refs, rows):
        v = v_ref[...]
        o1 = jnp.dot(p_refs[0][...], v, preferred_element_type=F32)
        o2 = jnp.dot(p_refs[1][...], v, preferred_element_type=F32)
        o = o1 * (1.0 / l_refs[0][...]) - o2 * (lam / l_refs[1][...])
        o = o * lax.rsqrt(jnp.mean(o * o, axis=-1, keepdims=True) + EPS) * hg_ref[...] * (1.0 - lambda_init)
        o_ref[rows, :] = (o * _silu(gate_ref[rows, :].astype(F32))).astype(o_ref.dtype)

    first_rows, second_rows = pl.ds(0, TQ), pl.ds(TQ, TQ)
    values((pa1, pa2), (la1, la2), first_rows)
    numerators((sb1, sb2), (pb1, pb2), (lb1, lb2))
    scores(tile0, first_rows, (sa1, sa2))
    values((pb1, pb2), (lb1, lb2), second_rows)
    numerators((sa1, sa2), (pa1, pa2), (la1, la2))
    scores(tile0 + 1, second_rows, (sb1, sb2))


def _attention(tab, hproj, nb, head_gain, lq1, lk1, lq2, lk2, *, batch, seq, lambda_init):
    h3 = hproj.reshape(batch, seq, hproj.shape[-1])
    kb, vb, gb = QK_WIDTH // V_DIM, 2 * QK_WIDTH // V_DIM, (2 * QK_WIDTH + ATTN_WIDTH) // V_DIM
    pairs_per_head = seq // (2 * TQ)
    n_pairs = batch * HEADS * pairs_per_head

    def coords(p):
        return p // (HEADS * pairs_per_head), (p // pairs_per_head) % HEADS, p % pairs_per_head

    def score_pair(t):
        return coords(jnp.minimum(t, n_pairs - 1))

    def value_pair(t):
        return coords(jnp.maximum(t - 1, 0))

    def q_map(t):
        b, h, ip = score_pair(t)
        return b, ip, h

    def k_map(t):
        b, h, _ = score_pair(t)
        return b, 0, kb + h

    def v_map(t):
        b, h, _ = value_pair(t)
        return b, 0, vb + h

    def gate_map(t):
        b, h, ip = value_pair(t)
        return b, ip, gb + h

    def out_map(t):
        b, h, ip = value_pair(t)
        return b, ip, h

    vec = pl.BlockSpec((1, HEAD_DIM), lambda t: (0, 0))
    s_buf = pltpu.VMEM((TQ, seq), F32)
    p_buf = pltpu.VMEM((TQ, seq), BF16)
    l_buf = pltpu.VMEM((TQ, 1), F32)
    vmem = (2 * (3 * 2 * TQ * V_DIM * 2 + 2 * seq * V_DIM * 2 + TQ * BAND_TILE * 4)
            + 2 * seq * V_DIM * 2 + 4 * TQ * seq * (4 + 2) + 4 * TQ * V7X_LANES * 4
            + 6 * TQ * seq * 4 + (2 << 20))
    return pl.pallas_call(
        functools.partial(_attn_kernel, seq=seq, n_pairs=n_pairs, pairs_per_head=pairs_per_head,
                          lambda_init=lambda_init),
        grid=(n_pairs + 1,),
        in_specs=[
            pl.BlockSpec(memory_space=pltpu.SMEM),
            pl.BlockSpec((None, 2 * TQ, V_DIM), q_map),
            pl.BlockSpec((None, seq, V_DIM), k_map),
            pl.BlockSpec((None, seq, V_DIM), v_map),
            pl.BlockSpec((None, 2 * TQ, V_DIM), gate_map),
            pl.BlockSpec((None, TQ, BAND_TILE), lambda t: (score_pair(t)[1], 0, 0)),
            pl.BlockSpec((None, 1, V_DIM), lambda t: (value_pair(t)[1], 0, 0)),
            vec, vec, vec, vec,
        ],
        out_specs=pl.BlockSpec((None, 2 * TQ, V_DIM), out_map),
        out_shape=jax.ShapeDtypeStruct((batch, seq, ATTN_WIDTH), BF16),
        scratch_shapes=[pltpu.VMEM((2, V_DIM, seq), BF16)] + [s_buf] * 4 + [p_buf] * 4 + [l_buf] * 4,
        compiler_params=_params(vmem, 1),
        name="diff_attention",
    )(tab, h3, h3, h3, h3, nb, head_gain.reshape(HEADS, 1, V_DIM), lq1, lk1, lq2, lk2)


def _proj_residual_kernel(y_ref, w_ref, x_ref, g_ref, o_ref, xn_ref):
    out = x_ref[...] + jnp.dot(y_ref[...], w_ref[...], preferred_element_type=F32)
    o_ref[...] = out
    xn_ref[...] = _rmsnorm_rows(out, g_ref[...]).astype(xn_ref.dtype)


def _proj_residual(y, w, x, next_gain, *, tm):
    m, k = y.shape
    n = w.shape[1]
    vmem = 2 * (tm * k * 2 + k * n * 2 + 2 * tm * n * 4 + tm * n * 2) + 3 * tm * n * 4 + (2 << 20)
    return pl.pallas_call(
        _proj_residual_kernel,
        grid=(m // tm,),
        in_specs=[
            pl.BlockSpec((tm, k), lambda i: (i, 0)),
            pl.BlockSpec((k, n), lambda i: (0, 0)),
            pl.BlockSpec((tm, n), lambda i: (i, 0)),
            pl.BlockSpec((1, n), lambda i: (0, 0)),
        ],
        out_specs=[pl.BlockSpec((tm, n), lambda i: (i, 0))] * 2,
        out_shape=[jax.ShapeDtypeStruct((m, n), F32), jax.ShapeDtypeStruct((m, n), BF16)],
        compiler_params=_params(vmem, 1),
        name="attn_out_proj",
    )(y, w, x, next_gain)


def _glu_kernel(xn_ref, wa_ref, wg_ref, wz_ref, ba_ref, bg_ref, bz_ref, wo_ref, u_ref, sz_ref, wob_ref,
                wb_ref):
    wob_ref[...] = wo_ref[...].astype(BF16)

    @pl.when(pl.program_id(1) == 0)
    def _():
        for part, w_ref in enumerate((wa_ref, wg_ref, wz_ref)):
            wb_ref[part] = w_ref[...].astype(BF16)

    xn = xn_ref[...]
    a = jnp.dot(xn, wb_ref[0], preferred_element_type=F32) + ba_ref[...]
    g = jnp.dot(xn, wb_ref[1], preferred_element_type=F32) + bg_ref[...]
    u_ref[...] = a * jax.nn.sigmoid(g)
    z = jnp.dot(xn, wb_ref[2], preferred_element_type=F32) + bz_ref[...]
    sz_ref[...] = _silu(z).astype(sz_ref.dtype)


def _glu(xn, w_all, b_all, wo_all, layer, *, tm, tn):
    m, d = xn.shape
    c = w_all.shape[2] // 3
    nj = c // tn
    grid = (nj, m // tm)
    wo_in, wo_out, wo_shape = _cast_slice_specs(wo_all, layer, grid)
    b3 = b_all.reshape(b_all.shape[0], 1, 3 * c)
    vmem = (2 * 3 * d * tn * 4 + 3 * d * tn * 2 + 2 * tm * d * 2 + 2 * tm * tn * (4 + 2)
            + 4 * tm * tn * 4 + (2 << 20))
    wspec = lambda part: pl.BlockSpec((None, d, tn), lambda jn, i: (layer, 0, part * nj + jn))
    bspec = lambda part: pl.BlockSpec((None, 1, tn), lambda jn, i: (layer, 0, part * nj + jn))
    return pl.pallas_call(
        _glu_kernel,
        grid=grid,
        in_specs=[pl.BlockSpec((tm, d), lambda jn, i: (i, 0)),
                  wspec(0), wspec(1), wspec(2), bspec(0), bspec(1), bspec(2), wo_in],
        out_specs=[pl.BlockSpec((tm, tn), lambda jn, i: (i, jn))] * 2 + [wo_out],
        out_shape=[jax.ShapeDtypeStruct((m, c), F32), jax.ShapeDtypeStruct((m, c), BF16), wo_shape],
        scratch_shapes=[pltpu.VMEM((3, d, tn), BF16)],
        compiler_params=_params(vmem, 2),
        name="conv_in_glu",
    )(xn, w_all, w_all, w_all, b3, b3, b3, wo_all)


def _conv_out_kernel(u_ref, up_ref, un_ref, sz_ref, dw8_ref, dwb_ref, lng_ref, lnb_ref, w_ref, bo_ref,
                     x_ref, g_ref, *refs, ts, tiles_per_seq, last_layer):
    out_refs, (ext_ref, y_ref) = refs[:-2], refs[-2:]
    i = pl.program_id(0)
    pos = i % tiles_per_seq
    c = u_ref.shape[1]
    sub = V7X_SUBLANES

    n_lane_chunks = c // V7X_LANES
    head = jnp.where(pos == 0, 0.0, up_ref[...])
    tail = jnp.where(pos == tiles_per_seq - 1, 0.0, un_ref[...])
    for cb in range(n_lane_chunks):
        lanes = slice(cb * V7X_LANES, (cb + 1) * V7X_LANES)
        ext_ref[cb, 0:HALO, :] = head[:, lanes]
        ext_ref[cb, HALO:HALO + ts, :] = u_ref[:, lanes]
        ext_ref[cb, HALO + ts:HALO + ts + HALO, :] = tail[:, lanes]

    first = HALO - CONV_HALF
    group = sub * CONV_STRIDE

    def conv_block(idx, carry):
        cb = idx % n_lane_chunks
        lanes = pl.ds(pl.multiple_of(cb * V7X_LANES, V7X_LANES), V7X_LANES)
        bias = jnp.broadcast_to(dwb_ref[:, lanes], (sub, V7X_LANES))
        taps = [dw8_ref[pl.ds(sub * tap, sub), lanes] for tap in range(CONV_KERNEL)]
        for g in range(CONV_GROUPS):
            g0 = ((idx // n_lane_chunks) * CONV_GROUPS + g) * group
            acc = [[bias, None] for _ in range(CONV_STRIDE)]
            for k in range(CONV_STRIDE + CONV_KERNEL - 1):
                src = ext_ref[cb, pl.ds(g0 + first + k, sub, stride=CONV_STRIDE), :]
                for j in range(CONV_STRIDE):
                    tap = k - j
                    if 0 <= tap < CONV_KERNEL:
                        prod = src * taps[tap]
                        part = acc[j]
                        part[tap % 2] = prod if part[tap % 2] is None else part[tap % 2] + prod
            for j in range(CONV_STRIDE):
                y_ref[cb, pl.ds(g0 + j, sub, stride=CONV_STRIDE), :] = acc[j][0] + acc[j][1]
        return carry

    lax.fori_loop(0, (ts // (group * CONV_GROUPS)) * n_lane_chunks, conv_block, 0)

    y = jnp.concatenate([y_ref[cb] for cb in range(n_lane_chunks)], axis=1)
    mu = jnp.mean(y, axis=-1, keepdims=True)
    yc = y - mu
    var = jnp.mean(yc * yc, axis=-1, keepdims=True)
    act = yc * lax.rsqrt(var + LN_EPS) * lng_ref[...] + lnb_ref[...]
    act = _silu(act) * sz_ref[...].astype(F32)
    out = x_ref[...] + jnp.dot(act.astype(BF16), w_ref[...], preferred_element_type=F32) + bo_ref[...]
    if not last_layer:
        out_refs[0][...] = out
    out_refs[-1][...] = _rmsnorm_rows(out, g_ref[...]).astype(out_refs[-1].dtype)


def _conv_out(u, sz, dw, dw_b, ln_g, ln_b, w, b_out, x, gain, *, seq, ts, last_layer):
    m, c = u.shape
    d = w.shape[1]
    tiles_per_seq = seq // ts
    hb = ts // HALO
    last_halo_block = m // HALO - 1
    dw8 = jnp.broadcast_to(dw[:, None, :], (CONV_KERNEL, V7X_SUBLANES, c)).reshape(CONV_KERNEL * V7X_SUBLANES, c)
    row = lambda width: pl.BlockSpec((1, width), lambda i: (0, 0))
    tile = pl.BlockSpec((ts, d), lambda i: (i, 0))
    if last_layer:
        out_specs, out_shape, out_bytes = [tile], [jax.ShapeDtypeStruct((m, d), F32)], ts * d * 4
    else:
        out_specs = [tile, tile]
        out_shape = [jax.ShapeDtypeStruct((m, d), F32), jax.ShapeDtypeStruct((m, d), BF16)]
        out_bytes = ts * d * (4 + 2)
    vmem = (2 * (ts * c * 4 + 2 * HALO * c * 4 + ts * c * 2 + CONV_KERNEL * V7X_SUBLANES * c * 4 + c * d * 2
                 + ts * d * 4 + out_bytes) + (ts + 2 * HALO) * c * 4 + ts * c * 4 + 4 * ts * c * 4 + (2 << 20))
    return pl.pallas_call(
        functools.partial(_conv_out_kernel, ts=ts, tiles_per_seq=tiles_per_seq, last_layer=last_layer),
        grid=(m // ts,),
        in_specs=[
            pl.BlockSpec((ts, c), lambda i: (i, 0)),
            pl.BlockSpec((HALO, c), lambda i: (jnp.maximum(i * hb - 1, 0), 0)),
            pl.BlockSpec((HALO, c), lambda i: (jnp.minimum((i + 1) * hb, last_halo_block), 0)),
            pl.BlockSpec((ts, c), lambda i: (i, 0)),
            pl.BlockSpec((CONV_KERNEL * V7X_SUBLANES, c), lambda i: (0, 0)),
            row(c), row(c), row(c),
            pl.BlockSpec((c, d), lambda i: (0, 0)),
            row(d),
            tile,
            row(d),
        ],
        out_specs=out_specs,
        out_shape=out_shape,
        scratch_shapes=[pltpu.VMEM((c // V7X_LANES, ts + 2 * HALO, V7X_LANES), F32),
                        pltpu.VMEM((c // V7X_LANES, ts, V7X_LANES), F32)],
        compiler_params=_params(vmem, 1),
        name="conv_out_proj",
    )(u, u, u, sz, dw8, dw_b, ln_g, ln_b, w, b_out, x, gain)


def kernel(x, norm_gain, final_gain, rel_bias, attn_w_in, attn_w_out, attn_lq1, attn_lk1, attn_lq2,
           attn_lk2, attn_head_gain, conv_w_in, conv_b_in, conv_dw, conv_dw_b, conv_ln_g, conv_ln_b,
           conv_w_out, conv_b_out):
    batch, seq, d = x.shape
    m = batch * seq
    xf = x.reshape(m, d)
    row = lambda v: v.reshape(1, -1)
    tab = rel_bias.T
    nb = _bias_band(tab)
    gains = [row(norm_gain[layer]) for layer in range(DEPTH)] + [row(final_gain)]
    xn = _rmsnorm(xf, gains[0], tm=512)
    for layer in range(DEPTH):
        j = layer // 2
        if layer % 2 == 0:
            hproj, w_out = _proj(xn, attn_w_in, attn_w_out, j, n_scaled=QK_WIDTH,
                                 col_scale=HEAD_DIM ** -0.5 * LOG2E, tm=1024, tn=1024)
            og = _attention(tab, hproj, nb, attn_head_gain[j], row(attn_lq1[j]), row(attn_lk1[j]),
                            row(attn_lq2[j]), row(attn_lk2[j]), batch=batch, seq=seq,
                            lambda_init=_lambda_init(layer))
            xf, xn = _proj_residual(og.reshape(m, ATTN_WIDTH), w_out, xf, gains[layer + 1], tm=512)
        else:
            u, sz, w_out = _glu(xn, conv_w_in, conv_b_in, conv_w_out, j, tm=512, tn=512)
            outs = _conv_out(u, sz, conv_dw[j], row(conv_dw_b[j]), row(conv_ln_g[j]), row(conv_ln_b[j]),
                             w_out, row(conv_b_out[j]), xf, gains[layer + 1],
                             seq=seq, ts=256, last_layer=(layer == DEPTH - 1))
            if layer == DEPTH - 1:
                return outs[0].reshape(batch, seq, d)
            xf, xn = outs
```
